```python
import math
import jax, jax.numpy as jnp
from jax import lax
import numpy as np

D_MODEL = 1024
BATCH = 8
SEQ = 2048
DEPTH = 4

GRID_W = 64
CTX_LEN = 256
N_EVEN = (DEPTH + 1) // 2
N_ODD = DEPTH // 2
EPS = 1e-6

MLA_HEADS = 8
MLA_Q_RANK = 384
MLA_KV_RANK = 256
MLA_NOPE = 64
MLA_ROPE = 32
MLA_V = 64
MLA_SCALE = (MLA_NOPE + MLA_ROPE) ** -0.5
ROPE_THETA = 10000.0
Q_BLOCK = 128
SC_WIDTH = 512
SC_K = 3
HYB_SPLITS = [MLA_Q_RANK, MLA_Q_RANK + MLA_KV_RANK, MLA_Q_RANK + MLA_KV_RANK + MLA_ROPE,
              MLA_Q_RANK + MLA_KV_RANK + MLA_ROPE + SC_WIDTH,
              MLA_Q_RANK + MLA_KV_RANK + MLA_ROPE + 2 * SC_WIDTH]
HYB_IN = MLA_Q_RANK + MLA_KV_RANK + MLA_ROPE + 3 * SC_WIDTH
HYB_MIX = MLA_HEADS * MLA_V + SC_WIDTH
SSD_INNER = 2 * D_MODEL
SSD_HEADDIM = 64
SSD_HEADS = SSD_INNER // SSD_HEADDIM
SSD_GROUPS = 4
SSD_STATE = 128
SSD_CONV_K = 3
SSD_CHUNK = 128
SSD_CONV_DIM = SSD_INNER + 2 * SSD_GROUPS * SSD_STATE
SSD_IN = SSD_INNER + SSD_CONV_DIM + 2 * SSD_HEADS
D_FF = 2816
FFN_K = 3

kernel_name = 'hybrid_mla_shortconv_ssd_convffn_prefix_dit'


def _rmsnorm(x, w):
    x32 = x.astype(jnp.float32)
    y = x32 * lax.rsqrt(jnp.mean(x32 * x32, axis=-1, keepdims=True) + EPS)
    return (y * w.astype(jnp.float32)).astype(x.dtype)


def _dwconv(u, w, b=None):
    k = w.shape[0]
    pad = k // 2
    n = u.shape[1]
    up = jnp.pad(u, ((0, 0), (pad, pad), (0, 0)))
    y = up[:, 0:n] * w[0]
    for i in range(1, k):
        y = y + up[:, i:i + n] * w[i]
    if b is not None:
        y = y + b
    return y


def _axial_angles(rows):
    row = jnp.repeat(jnp.arange(rows, dtype=jnp.float32), GRID_W)
    col = jnp.tile(jnp.arange(GRID_W, dtype=jnp.float32), rows)
    nf = MLA_ROPE // 4
    inv = ROPE_THETA ** (-jnp.arange(nf, dtype=jnp.float32) / nf)
    ang = jnp.concatenate([row[:, None] * inv, col[:, None] * inv], axis=-1)
    return jnp.cos(ang), jnp.sin(ang)


def _rope(x, cos, sin):
    half = x.shape[-1] // 2
    x1, x2 = x[..., :half], x[..., half:]
    cos = cos.astype(x.dtype)
    sin = sin.astype(x.dtype)
    return jnp.concatenate([x1 * cos - x2 * sin, x1 * sin + x2 * cos], axis=-1)


def _attend(qn, qr, kn, kr, v):
    s = jnp.einsum('bqhd,bkhd->bhqk', qn, kn) + jnp.einsum('bqhr,bkr->bhqk', qr, kr)
    p = jax.nn.softmax(s.astype(jnp.float32) * MLA_SCALE, axis=-1).astype(v.dtype)
    return jnp.einsum('bhqk,bkhd->bqhd', p, v)


def _attend_blocked(qn, qr, kn, kr, v):
    b, n, h, _ = qn.shape
    nb = n // Q_BLOCK
    qn_b = qn.reshape(b, nb, Q_BLOCK, h, MLA_NOPE).transpose(1, 0, 2, 3, 4)
    qr_b = qr.reshape(b, nb, Q_BLOCK, h, MLA_ROPE).transpose(1, 0, 2, 3, 4)
    out = lax.map(lambda qs: _attend(qs[0], qs[1], kn, kr, v), (qn_b, qr_b))
    return out.transpose(1, 0, 2, 3, 4).reshape(b, n, h, MLA_V)


def _mla_qkv(cq, ckv, kr, q_norm, kv_norm, w_uq, w_ukv, cos, sin):
    b, n, _ = cq.shape
    q = (_rmsnorm(cq, q_norm) @ w_uq).reshape(b, n, MLA_HEADS, MLA_NOPE + MLA_ROPE)
    kv = (_rmsnorm(ckv, kv_norm) @ w_ukv).reshape(b, n, MLA_HEADS, MLA_NOPE + MLA_V)
    qn, qr = q[..., :MLA_NOPE], q[..., MLA_NOPE:]
    kn, v = kv[..., :MLA_NOPE], kv[..., MLA_NOPE:]
    if cos is not None:
        qr = _rope(qr, cos[:, None, :], sin[:, None, :])
        kr = _rope(kr, cos, sin)
    return qn, qr, kn, kr, v


def _mixer_attn_conv(h_lat, h_ctx, w_in, q_norm, kv_norm, w_uq, w_ukv, sconv_w, w_out, cos, sin, need_ctx_out):
    b, n, _ = h_lat.shape
    cq_l, ckv_l, kr_l, gb_l, gc_l, xv_l = jnp.split(h_lat @ w_in, HYB_SPLITS, axis=-1)
    cq_c, ckv_c, kr_c, gb_c, gc_c, xv_c = jnp.split(h_ctx @ w_in, HYB_SPLITS, axis=-1)
    qn_l, qr_l, kn_l, kr_l, v_l = _mla_qkv(cq_l, ckv_l, kr_l, q_norm, kv_norm, w_uq, w_ukv, cos, sin)
    qn_c, qr_c, kn_c, kr_c, v_c = _mla_qkv(cq_c, ckv_c, kr_c, q_norm, kv_norm, w_uq, w_ukv, None, None)
    kn_all = jnp.concatenate([kn_c, kn_l], axis=1)
    kr_all = jnp.concatenate([kr_c, kr_l], axis=1)
    v_all = jnp.concatenate([v_c, v_l], axis=1)
    attn_l = _attend_blocked(qn_l, qr_l, kn_all, kr_all, v_all).reshape(b, n, MLA_HEADS * MLA_V)
    sc_l = gb_l * _dwconv(gc_l * xv_l, sconv_w)
    out_l = jnp.concatenate([attn_l, sc_l], axis=-1) @ w_out
    if not need_ctx_out:
        return out_l, None
    attn_c = _attend(qn_c, qr_c, kn_c, kr_c, v_c).reshape(b, h_ctx.shape[1], MLA_HEADS * MLA_V)
    sc_c = gb_c * _dwconv(gc_c * xv_c, sconv_w)
    out_c = jnp.concatenate([attn_c, sc_c], axis=-1) @ w_out
    return out_l, out_c


def _segsum(a):
    q = a.shape[-1]
    cs = jnp.cumsum(a, axis=-1)
    diff = cs[..., :, None] - cs[..., None, :]
    mask = jnp.tril(jnp.ones((q, q), dtype=bool))
    return jnp.where(mask, diff, -jnp.inf)


def _ssd_scan(xdt, a, bm, cm, h0):
    b, l, h, p = xdt.shape
    g, n = bm.shape[2], bm.shape[3]
    r = h // g
    c = l // SSD_CHUNK
    x = xdt.reshape(b, c, SSD_CHUNK, g, r, p)
    a = a.astype(jnp.float32).reshape(b, c, SSD_CHUNK, g, r).transpose(0, 1, 3, 4, 2)
    bc = bm.reshape(b, c, SSD_CHUNK, g, n)
    cc = cm.reshape(b, c, SSD_CHUNK, g, n)
    a_cum = jnp.cumsum(a, axis=-1)
    decay_in = jnp.exp(_segsum(a))
    cb = jnp.einsum('bclgn,bcsgn->bcgls', cc, bc)
    y_diag = jnp.einsum('bcgls,bcgrls,bcsgrp->bclgrp', cb, decay_in, x)
    decay_states = jnp.exp(a_cum[..., -1:] - a_cum)
    states = jnp.einsum('bcsgn,bcgrs,bcsgrp->bcgrpn', bc, decay_states, x)
    states = jnp.concatenate([h0.reshape(b, 1, g, r, p, n).astype(states.dtype), states], axis=1)
    tot = jnp.pad(a_cum[..., -1], ((0, 0), (1, 0), (0, 0), (0, 0))).transpose(0, 2, 3, 1)
    decay_chunk = jnp.exp(_segsum(tot))
    new_states = jnp.einsum('bgrzc,bcgrpn->bzgrpn', decay_chunk, states)
    states_in, final = new_states[:, :-1], new_states[:, -1]
    y_off = jnp.einsum('bclgn,bcgrpn,bcgrl->bclgrp', cc, states_in, jnp.exp(a_cum))
    y = (y_diag + y_off).reshape(b, l, h, p)
    return y, final.reshape(b, h, p, n)


def _mixer_ssd(hs, h0_f, h0_b, w_in, conv_w, conv_b, a_log, dt_bias, d_skip, norm_w, w_out, need_out):
    b, n, _ = hs.shape
    z, xbc, dt = jnp.split(hs @ w_in, [SSD_INNER, SSD_INNER + SSD_CONV_DIM], axis=-1)
    xbc = jax.nn.silu(_dwconv(xbc, conv_w, conv_b))
    xs, bm, cm = jnp.split(xbc, [SSD_INNER, SSD_INNER + SSD_GROUPS * SSD_STATE], axis=-1)
    xs = xs.reshape(b, n, SSD_HEADS, SSD_HEADDIM)
    bm = bm.reshape(b, n, SSD_GROUPS, SSD_STATE)
    cm = cm.reshape(b, n, SSD_GROUPS, SSD_STATE)
    dt = jax.nn.softplus(dt.astype(jnp.float32).reshape(b, n, 2, SSD_HEADS) + dt_bias.astype(jnp.float32))
    a = -jnp.exp(a_log.astype(jnp.float32)) * dt
    flip = lambda t: jnp.flip(t, axis=1)
    y_f, hf = _ssd_scan(xs * dt[:, :, 0, :, None], a[:, :, 0], bm, cm, h0_f)
    y_b, hb = _ssd_scan(flip(xs * dt[:, :, 1, :, None]), flip(a[:, :, 1]), flip(bm), flip(cm), h0_b)
    if not need_out:
        return None, hf, hb
    y = (y_f + flip(y_b)).astype(xs.dtype) + d_skip[:, None] * xs
    y = _rmsnorm(y.reshape(b, n, SSD_INNER) * jax.nn.silu(z), norm_w)
    return y @ w_out, hf, hb


def _conv_ffn(h, w_up, conv_w, w_down):
    u = _dwconv(h @ w_up, conv_w)
    act, gate = jnp.split(u, 2, axis=-1)
    return (jax.nn.silu(act) * gate) @ w_down


def _modulation(cond, w, b):
    return jnp.split(jax.nn.silu(cond) @ w + b, 6, axis=-1)


def setup_inputs(seed: int = 0) -> dict:
    key = jax.random.key(seed)
    ks = iter(jax.random.split(key, 40))
    nrm = lambda shape, scale: jax.random.normal(next(ks), shape, jnp.float32) * scale
    gain = lambda shape: 1.0 + 0.05 * jax.random.normal(next(ks), shape, jnp.float32)
    d = D_MODEL
    inputs = {
        'x': nrm((BATCH, SEQ, d), 1.0),
        'c': nrm((BATCH, d), 1.0),
        'ctx': nrm((BATCH, CTX_LEN, d), 1.0),
        'c_ctx': nrm((d,), 1.0),
        'mod_w': nrm((DEPTH, d, 6 * d), d ** -0.5),
        'mod_b': nrm((DEPTH, 6 * d), 0.01),
        'norm_w': gain((DEPTH, 4, d)),
        'ffn_w_up': nrm((DEPTH, d, 2 * D_FF), d ** -0.5),
        'ffn_conv_w': nrm((DEPTH, FFN_K, 2 * D_FF), FFN_K ** -0.5),
        'ffn_w_down': nrm((DEPTH, D_FF, d), D_FF ** -0.5),
        'hyb_w_in': nrm((N_EVEN, d, HYB_IN), d ** -0.5),
        'mla_q_norm': gain((N_EVEN, MLA_Q_RANK)),
        'mla_kv_norm': gain((N_EVEN, MLA_KV_RANK)),
        'mla_w_uq': nrm((N_EVEN, MLA_Q_RANK, MLA_HEADS * (MLA_NOPE + MLA_ROPE)), MLA_Q_RANK ** -0.5),
        'mla_w_ukv': nrm((N_EVEN, MLA_KV_RANK, MLA_HEADS * (MLA_NOPE + MLA_V)), MLA_KV_RANK ** -0.5),
        'sconv_w': nrm((N_EVEN, SC_K, SC_WIDTH), SC_K ** -0.5),
        'hyb_w_out': nrm((N_EVEN, HYB_MIX, d), HYB_MIX ** -0.5),
        'ssd_w_in': nrm((N_ODD, d, SSD_IN), d ** -0.5),
        'ssd_conv_w': nrm((N_ODD, SSD_CONV_K, SSD_CONV_DIM), SSD_CONV_K ** -0.5),
        'ssd_conv_b': nrm((N_ODD, SSD_CONV_DIM), 0.02),
    }
    a_log = jnp.log(jax.random.uniform(next(ks), (N_ODD, 2, SSD_HEADS), jnp.float32, 1.0, 16.0))
    dt0 = jnp.exp(jax.random.uniform(next(ks), (N_ODD, 2, SSD_HEADS), jnp.float32, math.log(1e-3), math.log(1e-1)))
    inputs['ssd_a_log'] = a_log
    inputs['ssd_dt_bias'] = dt0 + jnp.log(-jnp.expm1(-dt0))
    inputs['ssd_d'] = gain((N_ODD, SSD_HEADS))
    inputs['ssd_norm'] = gain((N_ODD, SSD_INNER))
    inputs['ssd_w_out'] = nrm((N_ODD, SSD_INNER, d), SSD_INNER ** -0.5)
    return inputs


def reference(x, c, ctx, c_ctx, mod_w, mod_b, norm_w, ffn_w_up, ffn_conv_w, ffn_w_down,
              hyb_w_in, mla_q_norm, mla_kv_norm, mla_w_uq, mla_w_ukv, sconv_w, hyb_w_out,
              ssd_w_in, ssd_conv_w, ssd_conv_b, ssd_a_log, ssd_dt_bias, ssd_d, ssd_norm, ssd_w_out):
    b = x.shape[0]
    rows = x.shape[1] // GRID_W
    cos, sin = _axial_angles(rows)
    for l in range(DEPTH):
        last = l == DEPTH - 1
        i = l // 2
        sh1, sc1, g1, sh2, sc2, g2 = _modulation(c[:, None, :], mod_w[l], mod_b[l])
        sh1c, sc1c, g1c, sh2c, sc2c, g2c = _modulation(c_ctx, mod_w[l], mod_b[l])
        h_l = _rmsnorm(x, norm_w[l, 0]) * (1.0 + sc1) + sh1
        h_c = _rmsnorm(ctx, norm_w[l, 0]) * (1.0 + sc1c) + sh1c
        if l % 2 == 0:
            m_l, m_c = _mixer_attn_conv(h_l, h_c, hyb_w_in[i], mla_q_norm[i], mla_kv_norm[i], mla_w_uq[i],
                                        mla_w_ukv[i], sconv_w[i], hyb_w_out[i], cos, sin, not last)
        else:
            zeros = jnp.zeros((b, SSD_HEADS, SSD_HEADDIM, SSD_STATE), jnp.float32)
            m_c, hf, hb = _mixer_ssd(h_c, zeros, zeros, ssd_w_in[i], ssd_conv_w[i], ssd_conv_b[i], ssd_a_log[i],
                                     ssd_dt_bias[i], ssd_d[i], ssd_norm[i], ssd_w_out[i], not last)
            m_l, _, _ = _mixer_ssd(h_l, hf, hb, ssd_w_in[i], ssd_conv_w[i], ssd_conv_b[i], ssd_a_log[i],
                                   ssd_dt_bias[i], ssd_d[i], ssd_norm[i], ssd_w_out[i], True)
        x = x + g1 * _rmsnorm(m_l, norm_w[l, 1])
        f_l = _conv_ffn(_rmsnorm(x, norm_w[l, 2]) * (1.0 + sc2) + sh2, ffn_w_up[l], ffn_conv_w[l], ffn_w_down[l])
        x = x + g2 * _rmsnorm(f_l, norm_w[l, 3])
        if not last:
            ctx = ctx + g1c * _rmsnorm(m_c, norm_w[l, 1])
            f_c = _conv_ffn(_rmsnorm(ctx, norm_w[l, 2]) * (1.0 + sc2c) + sh2c, ffn_w_up[l], ffn_conv_w[l], ffn_w_down[l])
            ctx = ctx + g2c * _rmsnorm(f_c, norm_w[l, 3])
    return x
```

```python
import functools

import jax
import jax.numpy as jnp
from jax import lax
from jax.experimental import pallas as pl
from jax.experimental.pallas import tpu as pltpu

F32 = jnp.float32
BF16 = jnp.bfloat16

EPS = 1e-6
GRID_W = 64
ROPE_THETA = 10000.0
MLA_HEADS = 8
MLA_Q_RANK = 384
MLA_KV_RANK = 256
MLA_NOPE = 64
MLA_ROPE = 32
MLA_V = 64
MLA_SCALE = (MLA_NOPE + MLA_ROPE) ** -0.5
HEAD_PAD = 128
SC_WIDTH = 512
SSD_HEADDIM = 64
SSD_GROUPS = 4
SSD_STATE = 128
SSD_CHUNK = 128
LANES = 128
F32_SUBLANES = 8
BF16_SUBLANES = 16
VMEM_LIMIT = 56 * 1024 * 1024


def _cparams(n_grid):
    return pltpu.CompilerParams(dimension_semantics=("arbitrary",) * n_grid, vmem_limit_bytes=VMEM_LIMIT)


def _rms(x, w):
    return x * lax.rsqrt(jnp.mean(x * x, axis=-1, keepdims=True) + EPS) * w


def _silu(x):
    return x * jax.nn.sigmoid(x)


def _dot(a, b):
    return jnp.dot(a, b, preferred_element_type=F32)


def _dot_nt(a, b):
    return lax.dot_general(a, b, (((1,), (1,)), ((), ())), preferred_element_type=F32)


def _const_spec(shape):
    nd = len(shape)
    return pl.BlockSpec(shape, lambda *_: (0,) * nd, pipeline_mode=pl.Buffered(1))


def _row_spec(tr, width):
    return pl.BlockSpec((1, tr, width), lambda b, i: (b, i, 0))


def _halo_specs(tr, halo, seq, width):
    per, last = tr // halo, seq // halo - 1
    prev = pl.BlockSpec((1, halo, width), lambda b, i: (b, jnp.maximum(i * per - 1, 0), 0))
    nxt = pl.BlockSpec((1, halo, width), lambda b, i: (b, jnp.minimum((i + 1) * per, last), 0))
    return prev, nxt


def _valid_rows(tr, halo, seq):
    rid = lax.broadcasted_iota(jnp.int32, (tr + 2 * halo, 1), 0) + (pl.program_id(1) * tr - halo)
    return (rid >= 0) & (rid < seq)


def _conv3(u, w):
    rows = u.shape[0]
    return pltpu.roll(u, 1, 0) * w[0:1] + u * w[1:2] + pltpu.roll(u, rows - 1, 0) * w[2:3]


def _mod_kernel(cond_ref, w_ref, b_ref, o_ref):
    cond = _silu(cond_ref[...])
    o_ref[0] = jnp.dot(cond, w_ref[0], preferred_element_type=F32, precision=lax.Precision.HIGHEST) + b_ref[0]


def _modulation(cond, mod_w, mod_b):
    depth, d, n = mod_w.shape
    rows = cond.shape[0]
    tn = 1536
    return pl.pallas_call(
        _mod_kernel,
        out_shape=jax.ShapeDtypeStruct((depth, rows, n), F32),
        grid=(depth, n // tn),
        in_specs=[pl.BlockSpec((rows, d), lambda l, j: (0, 0)),
                  pl.BlockSpec((1, d, tn), lambda l, j: (l, 0, j)),
                  pl.BlockSpec((1, 1, tn), lambda l, j: (l, 0, j))],
        out_specs=pl.BlockSpec((1, rows, tn), lambda l, j: (l, 0, j)),
        compiler_params=_cparams(2),
        name="modulation",
    )(cond, mod_w, mod_b.reshape(depth, 1, n))


def _ffn_kernel(xp_ref, x_ref, xn_ref, mod_ref, nw_ref, wup_ref, cw_ref, wdn_ref, o_ref, h_scr, acc_scr,
                *, tr, halo, seq, nchunk, ck):
    x = x_ref[0]
    x_ext = jnp.concatenate([xp_ref[0], x, xn_ref[0]], axis=0)
    sh2, sc2, g2 = mod_ref[0, 3:4, :], mod_ref[0, 4:5, :], mod_ref[0, 5:6, :]
    h = _rms(x_ext, nw_ref[2:3, :]) * (1.0 + sc2) + sh2
    h_scr[...] = jnp.where(_valid_rows(tr, halo, seq), h, 0.0).astype(BF16)
    acc_scr[...] = jnp.zeros_like(acc_scr)

    def body(c, carry):
        u = _dot(h_scr[...], wup_ref[c])
        y = _conv3(u, cw_ref[c])[halo:halo + tr]
        a = _silu(y[:, :ck]) * y[:, ck:]
        acc_scr[...] += _dot(a.astype(BF16), wdn_ref[c])
        return carry

    lax.fori_loop(0, nchunk, body, 0)
    o_ref[0] = x + g2 * _rms(acc_scr[...], nw_ref[3:4, :])


def _ffn(x, mods, nw, wup, cw, wdn, *, tr):
    nb, seq, d = x.shape
    halo = F32_SUBLANES
    nchunk, _, ck2 = wup.shape
    prev, nxt = _halo_specs(tr, halo, seq, d)
    kern = functools.partial(_ffn_kernel, tr=tr, halo=halo, seq=seq, nchunk=nchunk, ck=ck2 // 2)
    return pl.pallas_call(
        kern,
        out_shape=jax.ShapeDtypeStruct((nb, seq, d), F32),
        grid=(nb, seq // tr),
        in_specs=[prev, _row_spec(tr, d), nxt,
                  pl.BlockSpec((1, 6, d), lambda b, i: (b, 0, 0)),
                  _const_spec(nw.shape), _const_spec(wup.shape), _const_spec(cw.shape), _const_spec(wdn.shape)],
        out_specs=_row_spec(tr, d),
        scratch_shapes=[pltpu.VMEM((tr + 2 * halo, d), BF16), pltpu.VMEM((tr, d), F32)],
        compiler_params=_cparams(2),
        name="ffn",
    )(x, x, x, mods, nw, wup, cw, wdn)


def _rope(x, c, sa, sb):
    w = x.shape[1]
    return x * c + pltpu.roll(x, w - MLA_ROPE // 2, 1) * sa + pltpu.roll(x, MLA_ROPE // 2, 1) * sb


def _hyb_in_kernel(x_ref, mod_ref, nw_ref, win_ref, qn_ref, kvn_ref, wuq_ref, wk_ref, wv_ref, vone_ref,
                   rc_ref, rsa_ref, rsb_ref, q_ref, k_ref, v_ref, gb_ref, u_ref):
    sh1, sc1 = mod_ref[0, 0:1, :], mod_ref[0, 1:2, :]
    h = (_rms(x_ref[0], nw_ref[0:1, :]) * (1.0 + sc1) + sh1).astype(BF16)
    o_kv, o_kr, o_gb = MLA_Q_RANK, MLA_Q_RANK + MLA_KV_RANK, MLA_Q_RANK + MLA_KV_RANK + HEAD_PAD
    o_gc, o_xv = o_gb + SC_WIDTH, o_gb + 2 * SC_WIDTH
    rc, rsa, rsb = rc_ref[...], rsa_ref[...], rsb_ref[...]
    tile = lambda t: jnp.concatenate([t] * MLA_HEADS, axis=1)

    cq = _dot(h, win_ref[:, 0:o_kv])
    qn = _rms(cq, qn_ref[...]).astype(BF16)
    q = _rope(_dot(qn, wuq_ref[...]), tile(rc), tile(rsa), tile(rsb))
    q_ref[0] = (q * MLA_SCALE).astype(BF16)

    ckv = _dot(h, win_ref[:, o_kv:o_kr])
    kvn = _rms(ckv, kvn_ref[...]).astype(BF16)
    kr = _rope(_dot(h, win_ref[:, o_kr:o_gb]), rc, rsa, rsb)
    k_ref[0] = (_dot(kvn, wk_ref[...]) + tile(kr)).astype(BF16)
    v_ref[0] = (_dot(kvn, wv_ref[...]) + vone_ref[...]).astype(BF16)

    gb_ref[0] = _dot(h, win_ref[:, o_gb:o_gc]).astype(BF16)
    gc = _dot(h, win_ref[:, o_gc:o_xv])
    xv = _dot(h, win_ref[:, o_xv:o_xv + SC_WIDTH])
    u_ref[0] = (gc * xv).astype(BF16)


def _hyb_in(x, mods, nw, p, rope_tabs, *, tr):
    nb, seq, d = x.shape
    hp = MLA_HEADS * HEAD_PAD
    consts = [nw, p["w_in"], p["q_norm"], p["kv_norm"], p["w_uq"], p["w_k"], p["w_v"], p["v_one"]]
    tab_spec = pl.BlockSpec((tr, HEAD_PAD), lambda b, i: (i, 0))
    out_w = [hp, hp, hp, SC_WIDTH, SC_WIDTH]
    return pl.pallas_call(
        _hyb_in_kernel,
        out_shape=[jax.ShapeDtypeStruct((nb, seq, w), BF16) for w in out_w],
        grid=(nb, seq // tr),
        in_specs=[_row_spec(tr, d), pl.BlockSpec((1, 6, d), lambda b, i: (b, 0, 0))]
                 + [_const_spec(a.shape) for a in consts] + [tab_spec] * 3,
        out_specs=[_row_spec(tr, w) for w in out_w],
        compiler_params=_cparams(2),
        name="hyb_in",
    )(x, mods, *consts, *rope_tabs)


def _attn_kernel(*refs, n_src):
    q_ref, o_ref = refs[0], refs[1 + 2 * n_src]
    q = q_ref[0]
    scores = [_dot_nt(q, refs[1 + 2 * s][0]) for s in range(n_src)]
    m = functools.reduce(jnp.maximum, [jnp.max(s, axis=-1, keepdims=True) for s in scores])
    o = None
    for s in range(n_src):
        pv = _dot(jnp.exp(scores[s] - m).astype(BF16), refs[2 + 2 * s][0])
        o = pv if o is None else o + pv
    o_ref[0] = (o / o[:, MLA_V:MLA_V + 1]).astype(BF16)


def _attention(q, kvs, *, tq):
    nb, seq, hp = q.shape
    in_specs = [pl.BlockSpec((1, tq, HEAD_PAD), lambda b, h, i: (b, i, h))]
    args = [q]
    for k, v in kvs:
        spec = pl.BlockSpec((1, k.shape[1], HEAD_PAD), lambda b, h, i: (b, 0, h))
        in_specs += [spec, spec]
        args += [k, v]
    return pl.pallas_call(
        functools.partial(_attn_kernel, n_src=len(kvs)),
        out_shape=jax.ShapeDtypeStruct((nb, seq, hp), BF16),
        grid=(nb, MLA_HEADS, seq // tq),
        in_specs=in_specs,
        out_specs=pl.BlockSpec((1, tq, HEAD_PAD), lambda b, h, i: (b, i, h)),
        compiler_params=_cparams(3),
        name="attention",
    )(*args)


def _hyb_out_kernel(attn_ref, gb_ref, up_ref, u_ref, un_ref, x_ref, mod_ref, nw_ref, cw_ref, woa_ref, wos_ref,
                    o_ref, *, tr, halo, seq):
    u_ext = jnp.concatenate([up_ref[0], u_ref[0], un_ref[0]], axis=0).astype(F32)
    u_ext = jnp.where(_valid_rows(tr, halo, seq), u_ext, 0.0)
    sc = gb_ref[0].astype(F32) * _conv3(u_ext, cw_ref[...])[halo:halo + tr]
    m = _dot(attn_ref[0], woa_ref[...]) + _dot(sc.astype(BF16), wos_ref[...])
    o_ref[0] = x_ref[0] + mod_ref[0, 2:3, :] * _rms(m, nw_ref[1:2, :])


def _hyb_out(attn, gb, u, x, mods, nw, p, *, tr):
    nb, seq, d = x.shape
    halo = BF16_SUBLANES
    prev, nxt = _halo_specs(tr, halo, seq, SC_WIDTH)
    consts = [nw, p["sconv_w"], p["w_oa"], p["w_os"]]
    return pl.pallas_call(
        functools.partial(_hyb_out_kernel, tr=tr, halo=halo, seq=seq),
        out_shape=jax.ShapeDtypeStruct((nb, seq, d), F32),
        grid=(nb, seq // tr),
        in_specs=[_row_spec(tr, attn.shape[2]), _row_spec(tr, SC_WIDTH), prev, _row_spec(tr, SC_WIDTH), nxt,
                  _row_spec(tr, d), pl.BlockSpec((1, 6, d), lambda b, i: (b, 0, 0))]
                 + [_const_spec(a.shape) for a in consts],
        out_specs=_row_spec(tr, d),
        compiler_params=_cparams(2),
        name="hyb_out",
    )(attn, gb, u, u, u, x, mods, *consts)


def _ssd_in_kernel(xp_ref, x_ref, xn_ref, mod_ref, nw_ref, wz_ref, wx_ref, wdt_ref, cw_ref, cb_ref, dtb_ref,
                   z_ref, xbc_ref, dt_ref, *, tr, halo, seq, cn):
    x_ext = jnp.concatenate([xp_ref[0], x_ref[0], xn_ref[0]], axis=0)
    sh1, sc1 = mod_ref[0, 0:1, :], mod_ref[0, 1:2, :]
    h = _rms(x_ext, nw_ref[0:1, :]) * (1.0 + sc1) + sh1
    h = jnp.where(_valid_rows(tr, halo, seq), h, 0.0).astype(BF16)
    mid = slice(halo, halo + tr)
    for c in range(wz_ref.shape[1] // cn):
        cols = slice(c * cn, (c + 1) * cn)
        z_ref[0, :, cols] = _dot(h, wz_ref[:, cols])[mid].astype(BF16)
    for c in range(wx_ref.shape[1] // cn):
        cols = slice(c * cn, (c + 1) * cn)
        y = _conv3(_dot(h, wx_ref[:, cols]), cw_ref[:, cols])[mid] + cb_ref[:, cols]
        xbc_ref[0, :, cols] = _silu(y).astype(BF16)
    dt = _dot(h, wdt_ref[...])[mid] + dtb_ref[...]
    dt_ref[0] = jnp.maximum(dt, 0.0) + jnp.log1p(jnp.exp(-jnp.abs(dt)))


def _ssd_in(x, mods, nw, p, *, tr):
    nb, seq, d = x.shape
    halo = F32_SUBLANES
    prev, nxt = _halo_specs(tr, halo, seq, d)
    consts = [nw, p["w_z"], p["w_xbc"], p["w_dt"], p["conv_w"], p["conv_b"], p["dt_bias"]]
    wz, wx, wdt = p["w_z"].shape[1], p["w_xbc"].shape[1], p["w_dt"].shape[1]
    return pl.pallas_call(
        functools.partial(_ssd_in_kernel, tr=tr, halo=halo, seq=seq, cn=512),
        out_shape=[jax.ShapeDtypeStruct((nb, seq, wz), BF16), jax.ShapeDtypeStruct((nb, seq, wx), BF16),
                   jax.ShapeDtypeStruct((nb, seq, wdt), F32)],
        grid=(nb, seq // tr),
        in_specs=[prev, _row_spec(tr, d), nxt, pl.BlockSpec((1, 6, d), lambda b, i: (b, 0, 0))]
                 + [_const_spec(a.shape) for a in consts],
        out_specs=[_row_spec(tr, wz), _row_spec(tr, wx), _row_spec(tr, wdt)],
        compiler_params=_cparams(2),
        name="ssd_in",
    )(x, x, x, mods, *consts)


def _split3(x):
    hi = x.astype(BF16)
    r1 = x - hi.astype(F32)
    mid = r1.astype(BF16)
    lo = (r1 - mid.astype(F32)).astype(BF16)
    return hi, mid, lo


def _ssd_scan_kernel(xs_ref, b_ref, c_ref, dt_ref, alog_ref, alogx_ref, dskip_ref, h0_ref, y_ref, hout_ref,
                     yacc, st_f, st_b, *, nc):
    q = SSD_CHUNK
    hpg = xs_ref.shape[2] // SSD_HEADDIM
    st_f[...] = h0_ref[0, 0, 0]
    st_b[...] = h0_ref[0, 0, 1]
    yacc[...] = xs_ref[0].astype(F32) * dskip_ref[0]

    ii = lax.broadcasted_iota(jnp.int32, (q, q), 0)
    jj = lax.broadcasted_iota(jnp.int32, (q, q), 1)
    tri_incl = jnp.where(ii >= jj, 1.0, 0.0).astype(BF16)
    lane = lax.broadcasted_iota(jnp.int32, (1, LANES), 1)
    a_neg = jnp.where(lane < 2 * hpg, -jnp.exp(alog_ref[0]), 0.0)
    lane_x = lax.broadcasted_iota(jnp.int32, (q, xs_ref.shape[2]), 1)
    even_head = (lane_x % (2 * SSD_HEADDIM)) < SSD_HEADDIM

    def chunk(ci, d, st):
        rows = pl.ds(pl.multiple_of(ci * q, q), q)
        xs = xs_ref[0, rows, :].astype(F32)
        bc, cc = b_ref[0, rows, :], c_ref[0, rows, :]
        dt = dt_ref[0, rows, :]
        a = dt * a_neg
        cs = functools.reduce(jnp.add, [_dot(tri_incl, part) for part in _split3(a)])
        sel = lax.broadcasted_iota(jnp.int32, (LANES, xs.shape[1]), 0)
        col = lax.broadcasted_iota(jnp.int32, (LANES, xs.shape[1]), 1)
        expand = jnp.where(sel == d * hpg + col // SSD_HEADDIM, 1.0, 0.0).astype(BF16)
        ex = _dot(jnp.concatenate(_split3(dt) + _split3(cs), axis=0), expand)
        dt_e = ex[0:q] + ex[q:2 * q] + ex[2 * q:3 * q]
        cs_e = ex[3 * q:4 * q] + ex[4 * q:5 * q] + ex[5 * q:6 * q]
        tot_e = cs_e[q - 1:q, :]
        xdt = xs * dt_e
        if d == 0:
            g = cs
            x_state = xdt * jnp.exp(tot_e - cs_e)
            y_scale = jnp.exp(cs_e)
            keep = ii >= jj
        else:
            g = a - cs
            ecs_e = cs_e - dt_e * (-jnp.exp(alogx_ref[0, 1:2, :]))
            x_state = xdt * jnp.exp(ecs_e)
            y_scale = jnp.exp(tot_e - ecs_e)
            keep = jj >= ii
        g_t = g.T
        cb = _dot_nt(cc, bc)
        x_even = jnp.where(even_head, xdt, 0.0).astype(BF16)
        x_odd = jnp.where(even_head, 0.0, xdt).astype(BF16)
        pairs = []
        for pr in range(hpg // 2):
            blk = slice(pr * 2 * SSD_HEADDIM, (pr + 1) * 2 * SSD_HEADDIM)
            acc = None
            for r, xsel in ((2 * pr, x_even), (2 * pr + 1, x_odd)):
                k = d * hpg + r
                diff = g[:, k:k + 1] - g_t[k:k + 1, :]
                w = (cb * jnp.exp(jnp.where(keep, diff, -jnp.inf))).astype(BF16)
                part = _dot(w, xsel[:, blk])
                acc = part if acc is None else acc + part
            pairs.append(acc)
        y_diag = jnp.concatenate(pairs, axis=1)
        y_off = _dot(cc, st[...].astype(BF16)) * y_scale
        yacc[rows, :] += y_diag + y_off
        b_t = bc.astype(F32).T.astype(BF16)
        st[...] = jnp.exp(tot_e) * st[...] + _dot(b_t, x_state.astype(BF16))

    def body(ci, carry):
        chunk(ci, 0, st_f)
        chunk(nc - 1 - ci, 1, st_b)
        return carry

    lax.fori_loop(0, nc, body, 0)
    y_ref[0] = yacc[...].astype(BF16)
    hout_ref[0, 0, 0] = st_f[...]
    hout_ref[0, 0, 1] = st_b[...]


def _ssd_scan(xbc, dt, h0, p):
    nb, seq, _ = xbc.shape
    gw = h0.shape[4]
    n_x = SSD_GROUPS * gw // SSD_STATE
    st_spec = pl.BlockSpec((1, 1, 2, SSD_STATE, gw), lambda b, g: (b, g, 0, 0, 0))
    return pl.pallas_call(
        functools.partial(_ssd_scan_kernel, nc=seq // SSD_CHUNK),
        out_shape=[jax.ShapeDtypeStruct((nb, seq, SSD_GROUPS * gw), BF16), jax.ShapeDtypeStruct(h0.shape, F32)],
        grid=(nb, SSD_GROUPS),
        in_specs=[pl.BlockSpec((1, seq, gw), lambda b, g: (b, 0, g)),
                  pl.BlockSpec((1, seq, SSD_STATE), lambda b, g: (b, 0, n_x + g)),
                  pl.BlockSpec((1, seq, SSD_STATE), lambda b, g: (b, 0, n_x + SSD_GROUPS + g)),
                  pl.BlockSpec((1, seq, LANES), lambda b, g: (b, 0, g)),
                  pl.BlockSpec((1, 1, LANES), lambda b, g: (g, 0, 0)),
                  pl.BlockSpec((1, 2, gw), lambda b, g: (g, 0, 0)),
                  pl.BlockSpec((1, 1, gw), lambda b, g: (g, 0, 0)),
                  st_spec],
        out_specs=[pl.BlockSpec((1, seq, gw), lambda b, g: (b, 0, g)), st_spec],
        scratch_shapes=[pltpu.VMEM((seq, gw), F32), pltpu.VMEM((SSD_STATE, gw), F32),
                        pltpu.VMEM((SSD_STATE, gw), F32)],
        compiler_params=_cparams(2),
        name="ssd_scan",
    )(xbc, xbc, xbc, dt, p["a_log"], p["a_log_x"], p["d_skip"], h0)


def _ssd_out_kernel(y_ref, z_ref, x_ref, mod_ref, nw_ref, sn_ref, wo_ref, o_ref):
    y = y_ref[0].astype(F32) * _silu(z_ref[0].astype(F32))
    m = _dot(_rms(y, sn_ref[...]).astype(BF16), wo_ref[...])
    o_ref[0] = x_ref[0] + mod_ref[0, 2:3, :] * _rms(m, nw_ref[1:2, :])


def _ssd_out(y, z, x, mods, nw, p, *, tr):
    nb, seq, d = x.shape
    consts = [nw, p["norm"], p["w_out"]]
    return pl.pallas_call(
        _ssd_out_kernel,
        out_shape=jax.ShapeDtypeStruct((nb, seq, d), F32),
        grid=(nb, seq // tr),
        in_specs=[_row_spec(tr, y.shape[2]), _row_spec(tr, z.shape[2]), _row_spec(tr, d),
                  pl.BlockSpec((1, 6, d), lambda b, i: (b, 0, 0))] + [_const_spec(a.shape) for a in consts],
        out_specs=_row_spec(tr, d),
        compiler_params=_cparams(2),
        name="ssd_out",
    )(y, z, x, mods, *consts)


def _prep_ffn(w_up, conv_w, w_down, ck):
    d, two_ff = w_up.shape
    nchunk = two_ff // 2 // ck
    wup = w_up.reshape(d, 2, nchunk, ck).transpose(2, 0, 1, 3).reshape(nchunk, d, 2 * ck).astype(BF16)
    cw = conv_w.reshape(conv_w.shape[0], 2, nchunk, ck).transpose(2, 0, 1, 3).reshape(nchunk, -1, 2 * ck)
    return wup, cw, w_down.reshape(nchunk, ck, d).astype(BF16)


def _prep_hyb(w_in, q_norm, kv_norm, w_uq, w_ukv, sconv_w, w_out):
    d = w_in.shape[0]
    o_kr = MLA_Q_RANK + MLA_KV_RANK
    kr_block = jnp.pad(w_in[:, o_kr:o_kr + MLA_ROPE], ((0, 0), (MLA_NOPE, HEAD_PAD - MLA_NOPE - MLA_ROPE)))
    w_in_p = jnp.concatenate([w_in[:, :o_kr], kr_block, w_in[:, o_kr + MLA_ROPE:]], axis=1)
    pad_heads = lambda w, width: jnp.pad(w, ((0, 0), (0, 0), (0, HEAD_PAD - width))).reshape(w.shape[0], -1)
    qk = MLA_NOPE + MLA_ROPE
    w_uq_p = pad_heads(w_uq.reshape(MLA_Q_RANK, MLA_HEADS, qk), qk)
    w_ukv_h = w_ukv.reshape(MLA_KV_RANK, MLA_HEADS, MLA_NOPE + MLA_V)
    w_k = pad_heads(w_ukv_h[..., :MLA_NOPE], MLA_NOPE)
    w_v = pad_heads(w_ukv_h[..., MLA_NOPE:], MLA_V)
    v_one = jnp.tile((jnp.arange(HEAD_PAD) == MLA_V).astype(F32), MLA_HEADS)[None, :]
    n_attn = MLA_HEADS * MLA_V
    w_oa = jnp.pad(w_out[:n_attn].reshape(MLA_HEADS, MLA_V, d), ((0, 0), (0, HEAD_PAD - MLA_V), (0, 0)))
    return dict(w_in=w_in_p.astype(BF16), q_norm=q_norm[None, :], kv_norm=kv_norm[None, :],
                w_uq=w_uq_p.astype(BF16), w_k=w_k.astype(BF16), w_v=w_v.astype(BF16), v_one=v_one,
                sconv_w=sconv_w, w_oa=w_oa.reshape(MLA_HEADS * HEAD_PAD, d).astype(BF16),
                w_os=w_out[n_attn:].astype(BF16))


def _prep_ssd(w_in, conv_w, conv_b, a_log, dt_bias, d_skip, norm_w, w_out):
    d = w_in.shape[0]
    inner = norm_w.shape[0]
    heads = a_log.shape[1]
    hpg = heads // SSD_GROUPS
    n_xbc = conv_w.shape[1]
    per_group = lambda t: jnp.pad(t.reshape(-1, 2, SSD_GROUPS, hpg).transpose(0, 2, 1, 3).reshape(-1, SSD_GROUPS, 2 * hpg),
                                  ((0, 0), (0, 0), (0, LANES - 2 * hpg)))
    w_dt = per_group(w_in[:, inner + n_xbc:]).reshape(d, SSD_GROUPS * LANES)
    a_g = a_log.reshape(2, SSD_GROUPS, hpg).transpose(1, 0, 2)
    return dict(w_z=w_in[:, :inner].astype(BF16), w_xbc=w_in[:, inner:inner + n_xbc].astype(BF16),
                w_dt=w_dt.astype(BF16), conv_w=conv_w, conv_b=conv_b[None, :],
                dt_bias=per_group(dt_bias.reshape(1, -1)).reshape(1, SSD_GROUPS * LANES),
                a_log=per_group(a_log.reshape(1, -1))[0][:, None, :],
                a_log_x=jnp.repeat(a_g, SSD_HEADDIM, axis=2),
                d_skip=jnp.repeat(d_skip.reshape(SSD_GROUPS, 1, hpg), SSD_HEADDIM, axis=2),
                norm=norm_w[None, :], w_out=w_out.astype(BF16))


def _rope_tables(rows):
    row = jnp.repeat(jnp.arange(rows, dtype=F32), GRID_W)
    col = jnp.tile(jnp.arange(GRID_W, dtype=F32), rows)
    nf = MLA_ROPE // 4
    inv = ROPE_THETA ** (-jnp.arange(nf, dtype=F32) / nf)
    ang = jnp.concatenate([row[:, None] * inv, col[:, None] * inv], axis=-1)
    cos, sin = jnp.cos(ang), jnp.sin(ang)
    n = ang.shape[0]
    half = MLA_ROPE // 2
    ones, zeros = jnp.ones((n, MLA_NOPE), F32), jnp.zeros((n, MLA_NOPE), F32)
    tail1, tail0 = jnp.ones((n, HEAD_PAD - MLA_NOPE - MLA_ROPE), F32), jnp.zeros((n, HEAD_PAD - MLA_NOPE - MLA_ROPE), F32)
    zh = jnp.zeros((n, half), F32)
    return (jnp.concatenate([ones, cos, cos, tail1], axis=1),
            jnp.concatenate([zeros, -sin, zh, tail0], axis=1),
            jnp.concatenate([zeros, zh, sin, tail0], axis=1))


def _identity_tables(n):
    return jnp.ones((n, HEAD_PAD), F32), jnp.zeros((n, HEAD_PAD), F32), jnp.zeros((n, HEAD_PAD), F32)


def _row_tile(seq):
    return min(seq, 512)


def kernel(x, c, ctx, c_ctx, mod_w, mod_b, norm_w, ffn_w_up, ffn_conv_w, ffn_w_down, hyb_w_in, mla_q_norm, mla_kv_norm, mla_w_uq, mla_w_ukv, sconv_w, hyb_w_out, ssd_w_in, ssd_conv_w, ssd_conv_b, ssd_a_log, ssd_dt_bias, ssd_d, ssd_norm, ssd_w_out):
    nb, seq, d = x.shape
    n_ctx = ctx.shape[1]
    depth = mod_w.shape[0]
    tr_l, tr_c = _row_tile(seq), _row_tile(n_ctx)
    tq_l, tq_c = min(seq, 256), min(n_ctx, 256)

    n_rows = -(-(nb + 1) // F32_SUBLANES) * F32_SUBLANES
    cond = jnp.concatenate([c, c_ctx[None, :], jnp.zeros((n_rows - nb - 1, d), F32)], axis=0)
    mods = _modulation(cond, mod_w, mod_b).reshape(depth, n_rows, 6, d)

    tabs_l = _rope_tables(seq // GRID_W)
    tabs_c = _identity_tables(n_ctx)
    heads = ssd_a_log.shape[2]
    h0 = jnp.zeros((nb, SSD_GROUPS, 2, SSD_STATE, heads // SSD_GROUPS * SSD_HEADDIM), F32)

    for l in range(depth):
        last = l == depth - 1
        i = l // 2
        mod_l = mods[l, :nb]
        mod_c = jnp.broadcast_to(mods[l, nb:nb + 1], (nb, 6, d))
        nw = norm_w[l]
        if l % 2 == 0:
            p = _prep_hyb(hyb_w_in[i], mla_q_norm[i], mla_kv_norm[i], mla_w_uq[i], mla_w_ukv[i], sconv_w[i],
                          hyb_w_out[i])
            q_l, k_l, v_l, gb_l, u_l = _hyb_in(x, mod_l, nw, p, tabs_l, tr=tr_l)
            q_c, k_c, v_c, gb_c, u_c = _hyb_in(ctx, mod_c, nw, p, tabs_c, tr=tr_c)
            attn_l = _attention(q_l, [(k_c, v_c), (k_l, v_l)], tq=tq_l)
            x = _hyb_out(attn_l, gb_l, u_l, x, mod_l, nw, p, tr=tr_l)
            if not last:
                attn_c = _attention(q_c, [(k_c, v_c)], tq=tq_c)
                ctx = _hyb_out(attn_c, gb_c, u_c, ctx, mod_c, nw, p, tr=tr_c)
        else:
            p = _prep_ssd(ssd_w_in[i], ssd_conv_w[i], ssd_conv_b[i], ssd_a_log[i], ssd_dt_bias[i], ssd_d[i],
                          ssd_norm[i], ssd_w_out[i])
            z_c, xbc_c, dt_c = _ssd_in(ctx, mod_c, nw, p, tr=tr_c)
            y_c, h_fin = _ssd_scan(xbc_c, dt_c, h0, p)
            z_l, xbc_l, dt_l = _ssd_in(x, mod_l, nw, p, tr=tr_l)
            y_l, _ = _ssd_scan(xbc_l, dt_l, h_fin, p)
            x = _ssd_out(y_l, z_l, x, mod_l, nw, p, tr=tr_l)
            if not last:
                ctx = _ssd_out(y_c, z_c, ctx, mod_c, nw, p, tr=tr_c)
        wup, cw, wdn = _prep_ffn(ffn_w_up[l], ffn_conv_w[l], ffn_w_down[l], ck=256)
        x = _ffn(x, mod_l, nw, wup, cw, wdn, tr=tr_l)
        if not last:
            ctx = _ffn(ctx, mod_c, nw, wup, cw, wdn, tr=tr_c)
    return x
```

```python
import functools

import jax
import jax.numpy as jnp
from jax import lax
from jax.experimental import pallas as pl
from jax.experimental.pallas import tpu as pltpu

F32 = jnp.float32
BF16 = jnp.bfloat16

EPS = 1e-6
LOG2_E = 1.4426950408889634
GRID_W = 64
ROPE_THETA = 10000.0
MLA_HEADS = 8
MLA_Q_RANK = 384
MLA_KV_RANK = 256
MLA_NOPE = 64
MLA_ROPE = 32
MLA_V = 64
MLA_SCALE = (MLA_NOPE + MLA_ROPE) ** -0.5
HEAD_PAD = 128
SC_WIDTH = 512
SSD_HEADDIM = 64
SSD_GROUPS = 4
SSD_STATE = 128
SSD_CHUNK = 128
FFN_CHUNK = 256
LANES = 128
F32_SUBLANES = 8
BF16_SUBLANES = 16
VMEM_LIMIT = 56 * 1024 * 1024


def _cparams(n_grid):
    return pltpu.CompilerParams(dimension_semantics=("arbitrary",) * n_grid, vmem_limit_bytes=VMEM_LIMIT)


def _rms(x, w):
    return x * lax.rsqrt(jnp.mean(x * x, axis=-1, keepdims=True) + EPS) * w


def _silu(x):
    return x * jax.nn.sigmoid(x)


def _dot(a, b):
    return jnp.dot(a, b, preferred_element_type=F32)


def _dot_nt(a, b):
    return lax.dot_general(a, b, (((1,), (1,)), ((), ())), preferred_element_type=F32)


def _const_spec(shape):
    nd = len(shape)
    return pl.BlockSpec(shape, lambda *_: (0,) * nd, pipeline_mode=pl.Buffered(1))


def _row_spec(tr, width):
    return pl.BlockSpec((1, tr, width), lambda b, i: (b, i, 0))


def _halo_specs(tr, halo, seq, width):
    per, last = tr // halo, seq // halo - 1
    prev = pl.BlockSpec((1, halo, width), lambda b, i: (b, jnp.maximum(i * per - 1, 0), 0))
    nxt = pl.BlockSpec((1, halo, width), lambda b, i: (b, jnp.minimum((i + 1) * per, last), 0))
    return prev, nxt


def _valid_rows(tr, halo, seq):
    rid = lax.broadcasted_iota(jnp.int32, (tr + 2 * halo, 1), 0) + (pl.program_id(1) * tr - halo)
    return (rid >= 0) & (rid < seq)


def _conv3(u, w):
    rows = u.shape[0]
    return pltpu.roll(u, 1, 0) * w[0:1] + u * w[1:2] + pltpu.roll(u, rows - 1, 0) * w[2:3]


def _mod_kernel(cond_ref, w_ref, b_ref, o_ref):
    cond = _silu(cond_ref[...])
    o_ref[0] = jnp.dot(cond, w_ref[0], preferred_element_type=F32, precision=lax.Precision.HIGHEST) + b_ref[0]


def _modulation(cond, mod_w, mod_b):
    depth, d, n = mod_w.shape
    rows = cond.shape[0]
    tn = 1536
    return pl.pallas_call(
        _mod_kernel,
        out_shape=jax.ShapeDtypeStruct((depth, rows, n), F32),
        grid=(depth, n // tn),
        in_specs=[pl.BlockSpec((rows, d), lambda l, j: (0, 0)),
                  pl.BlockSpec((1, d, tn), lambda l, j: (l, 0, j)),
                  pl.BlockSpec((1, 1, tn), lambda l, j: (l, 0, j))],
        out_specs=pl.BlockSpec((1, rows, tn), lambda l, j: (l, 0, j)),
        compiler_params=_cparams(2),
        name="modulation",
    )(cond, mod_w, mod_b.reshape(depth, 1, n))


def _ffn_kernel(xp_ref, x_ref, xn_ref, mod_ref, nw_ref, wup_ref, cw_ref, wdn_ref, o_ref, h_scr, u_scr, acc_scr,
                *, tr, halo, seq, nchunk, ck):
    x = x_ref[0]
    x_ext = jnp.concatenate([xp_ref[0], x, xn_ref[0]], axis=0)
    sh2, sc2, g2 = mod_ref[0, 3:4, :], mod_ref[0, 4:5, :], mod_ref[0, 5:6, :]
    h = _rms(x_ext, nw_ref[2:3, :]) * (1.0 + sc2) + sh2
    h_scr[...] = jnp.where(_valid_rows(tr, halo, seq), h, 0.0).astype(BF16)

    dff = wdn_ref.shape[0]

    def up(c, slot):
        for half in range(2):
            cols = slice(half * dff + c * ck, half * dff + (c + 1) * ck)
            u_scr[slot, half] = _dot(h_scr[...], wup_ref[:, cols])

    def down(c, slot, first=False):
        ys = []
        for half in range(2):
            cols = slice(half * dff + c * ck, half * dff + (c + 1) * ck)
            ys.append(_conv3(u_scr[slot, half], cw_ref[:, cols])[halo:halo + tr])
        a = _silu(ys[0]) * ys[1]
        part = _dot(a.astype(BF16), wdn_ref[c * ck:(c + 1) * ck, :])
        acc_scr[...] = part if first else acc_scr[...] + part

    up(0, 0)
    for c in range(nchunk):
        if c + 1 < nchunk:
            up(c + 1, (c + 1) % 2)
        down(c, c % 2, first=c == 0)
    o_ref[0] = x + g2 * _rms(acc_scr[...], nw_ref[3:4, :])


def _ffn(x, mods, nw, wup, cw, wdn, *, tr, ck):
    nb, seq, d = x.shape
    halo = F32_SUBLANES
    nchunk = wdn.shape[0] // ck
    prev, nxt = _halo_specs(tr, halo, seq, d)
    kern = functools.partial(_ffn_kernel, tr=tr, halo=halo, seq=seq, nchunk=nchunk, ck=ck)
    return pl.pallas_call(
        kern,
        out_shape=jax.ShapeDtypeStruct((nb, seq, d), F32),
        grid=(nb, seq // tr),
        in_specs=[prev, _row_spec(tr, d), nxt,
                  pl.BlockSpec((1, 6, d), lambda b, i: (b, 0, 0)),
                  _const_spec(nw.shape), _const_spec(wup.shape), _const_spec(cw.shape), _const_spec(wdn.shape)],
        out_specs=_row_spec(tr, d),
        scratch_shapes=[pltpu.VMEM((tr + 2 * halo, d), BF16), pltpu.VMEM((2, 2, tr + 2 * halo, ck), F32),
                        pltpu.VMEM((tr, d), F32)],
        compiler_params=_cparams(2),
        name="ffn",
    )(x, x, x, mods, nw, wup, cw, wdn)


def _rope(x, c, sa, sb):
    w = x.shape[1]
    return x * c + pltpu.roll(x, w - MLA_ROPE // 2, 1) * sa + pltpu.roll(x, MLA_ROPE // 2, 1) * sb


def _hyb_in_kernel(x_ref, mod_ref, nw_ref, win_ref, qn_ref, kvn_ref, wuq_ref, wk_ref, wvt_ref,
                   rc_ref, rsa_ref, rsb_ref, q_ref, k_ref, vt_ref, gb_ref, u_ref):
    sh1, sc1 = mod_ref[0, 0:1, :], mod_ref[0, 1:2, :]
    h = (_rms(x_ref[0], nw_ref[0:1, :]) * (1.0 + sc1) + sh1).astype(BF16)
    o_kv, o_kr, o_gb = MLA_Q_RANK, MLA_Q_RANK + MLA_KV_RANK, MLA_Q_RANK + MLA_KV_RANK + HEAD_PAD
    o_gc, o_xv = o_gb + SC_WIDTH, o_gb + 2 * SC_WIDTH
    rc, rsa, rsb = rc_ref[...], rsa_ref[...], rsb_ref[...]
    tile = lambda t: jnp.concatenate([t] * MLA_HEADS, axis=1)

    cq = _dot(h, win_ref[:, 0:o_kv])
    qn = _rms(cq, qn_ref[...]).astype(BF16)
    q = _rope(_dot(qn, wuq_ref[...]), tile(rc), tile(rsa), tile(rsb))
    q_ref[0] = (q * MLA_SCALE).astype(BF16)

    ckv = _dot(h, win_ref[:, o_kv:o_kr])
    kvn = _rms(ckv, kvn_ref[...]).astype(BF16)
    kr = _rope(_dot(h, win_ref[:, o_kr:o_gb]), rc, rsa, rsb)
    k_ref[0] = (_dot(kvn, wk_ref[...]) + tile(kr)).astype(BF16)
    vt = _dot_nt(wvt_ref[...], kvn)
    row = lax.broadcasted_iota(jnp.int32, vt.shape, 0)
    vt_ref[0] = jnp.where(row % HEAD_PAD == MLA_V, 1.0, vt).astype(BF16)

    gb_ref[0] = _dot(h, win_ref[:, o_gb:o_gc]).astype(BF16)
    gc = _dot(h, win_ref[:, o_gc:o_xv])
    xv = _dot(h, win_ref[:, o_xv:o_xv + SC_WIDTH])
    u_ref[0] = (gc * xv).astype(BF16)


def _hyb_in(x, mods, nw, p, rope_tabs, *, tr):
    nb, seq, d = x.shape
    hp = MLA_HEADS * HEAD_PAD
    consts = [nw, p["w_in"], p["q_norm"], p["kv_norm"], p["w_uq"], p["w_k"], p["w_vt"]]
    tab_spec = pl.BlockSpec((tr, HEAD_PAD), lambda b, i: (i, 0))
    row_out = lambda w: (jax.ShapeDtypeStruct((nb, seq, w), BF16), _row_spec(tr, w))
    outs = [row_out(hp), row_out(hp),
            (jax.ShapeDtypeStruct((nb, hp, seq), BF16), pl.BlockSpec((1, hp, tr), lambda b, i: (b, 0, i))),
            row_out(SC_WIDTH), row_out(SC_WIDTH)]
    return pl.pallas_call(
        _hyb_in_kernel,
        out_shape=[o[0] for o in outs],
        grid=(nb, seq // tr),
        in_specs=[_row_spec(tr, d), pl.BlockSpec((1, 6, d), lambda b, i: (b, 0, 0))]
                 + [_const_spec(a.shape) for a in consts] + [tab_spec] * 3,
        out_specs=[o[1] for o in outs],
        compiler_params=_cparams(2),
        name="hyb_in",
    )(x, mods, *consts, *rope_tabs)


def _attn_kernel(*refs, n_src, n_sub):
    q_ref, o_ref = refs[0], refs[1 + 2 * n_src]
    sub = q_ref.shape[1] // n_sub
    def qk(t):
        q = q_ref[0, t * sub:(t + 1) * sub, :]
        return [_dot_nt(refs[1 + 2 * s][0], q) for s in range(n_src)]

    nxt = qk(0)
    for t in range(n_sub):
        rows = slice(t * sub, (t + 1) * sub)
        scores, nxt = nxt, (qk(t + 1) if t + 1 < n_sub else None)
        m = functools.reduce(jnp.maximum, [jnp.max(s, axis=0, keepdims=True) for s in scores])
        o_t = None
        for s in range(n_src):
            pv = _dot(refs[2 + 2 * s][0], jnp.exp(scores[s] - m).astype(BF16))
            o_t = pv if o_t is None else o_t + pv
        o_t = o_t / o_t[MLA_V:MLA_V + 1, :]
        o_ref[0, rows, :] = o_t.T.astype(BF16)


def _attention(q, kvs, *, tq):
    nb, seq, hp = q.shape
    in_specs = [pl.BlockSpec((1, tq, HEAD_PAD), lambda b, h, i: (b, i, h))]
    args = [q]
    for k, vt in kvs:
        in_specs += [pl.BlockSpec((1, k.shape[1], HEAD_PAD), lambda b, h, i: (b, 0, h)),
                     pl.BlockSpec((1, HEAD_PAD, k.shape[1]), lambda b, h, i: (b, h, 0))]
        args += [k, vt]
    return pl.pallas_call(
        functools.partial(_attn_kernel, n_src=len(kvs), n_sub=max(tq // 512, 1)),
        out_shape=jax.ShapeDtypeStruct((nb, seq, hp), BF16),
        grid=(nb, MLA_HEADS, seq // tq),
        in_specs=in_specs,
        out_specs=pl.BlockSpec((1, tq, HEAD_PAD), lambda b, h, i: (b, i, h)),
        compiler_params=_cparams(3),
        name="attention",
    )(*args)


def _hyb_out_kernel(attn_ref, gb_ref, up_ref, u_ref, un_ref, x_ref, mod_ref, nw_ref, cw_ref, woa_ref, wos_ref,
                    o_ref, *, tr, halo, seq):
    u_ext = jnp.concatenate([up_ref[0], u_ref[0], un_ref[0]], axis=0).astype(F32)
    u_ext = jnp.where(_valid_rows(tr, halo, seq), u_ext, 0.0)
    sc = gb_ref[0].astype(F32) * _conv3(u_ext, cw_ref[...])[halo:halo + tr]
    m = _dot(attn_ref[0], woa_ref[...]) + _dot(sc.astype(BF16), wos_ref[...])
    o_ref[0] = x_ref[0] + mod_ref[0, 2:3, :] * _rms(m, nw_ref[1:2, :])


def _hyb_out(attn, gb, u, x, mods, nw, p, *, tr):
    nb, seq, d = x.shape
    halo = BF16_SUBLANES
    prev, nxt = _halo_specs(tr, halo, seq, SC_WIDTH)
    consts = [nw, p["sconv_w"], p["w_oa"], p["w_os"]]
    return pl.pallas_call(
        functools.partial(_hyb_out_kernel, tr=tr, halo=halo, seq=seq),
        out_shape=jax.ShapeDtypeStruct((nb, seq, d), F32),
        grid=(nb, seq // tr),
        in_specs=[_row_spec(tr, attn.shape[2]), _row_spec(tr, SC_WIDTH), prev, _row_spec(tr, SC_WIDTH), nxt,
                  _row_spec(tr, d), pl.BlockSpec((1, 6, d), lambda b, i: (b, 0, 0))]
                 + [_const_spec(a.shape) for a in consts],
        out_specs=_row_spec(tr, d),
        compiler_params=_cparams(2),
        name="hyb_out",
    )(attn, gb, u, u, u, x, mods, *consts)


def _ssd_in_kernel(xp_ref, x_ref, xn_ref, mod_ref, nw_ref, wz_ref, wx_ref, wdt_ref, cw_ref, cb_ref, dtb_ref,
                   z_ref, xbc_ref, dt_ref, *, tr, halo, seq, cn):
    x_ext = jnp.concatenate([xp_ref[0], x_ref[0], xn_ref[0]], axis=0)
    sh1, sc1 = mod_ref[0, 0:1, :], mod_ref[0, 1:2, :]
    h = _rms(x_ext, nw_ref[0:1, :]) * (1.0 + sc1) + sh1
    h = jnp.where(_valid_rows(tr, halo, seq), h, 0.0).astype(BF16)
    mid = slice(halo, halo + tr)
    for c in range(wz_ref.shape[1] // cn):
        cols = slice(c * cn, (c + 1) * cn)
        z_ref[0, :, cols] = _dot(h, wz_ref[:, cols])[mid].astype(BF16)
    for c in range(wx_ref.shape[1] // cn):
        cols = slice(c * cn, (c + 1) * cn)
        y = _conv3(_dot(h, wx_ref[:, cols]), cw_ref[:, cols])[mid] + cb_ref[:, cols]
        xbc_ref[0, :, cols] = _silu(y).astype(BF16)
    dt = _dot(h, wdt_ref[...])[mid] + dtb_ref[...]
    dt_ref[0] = jnp.maximum(dt, 0.0) + jnp.log1p(jnp.exp(-jnp.abs(dt)))


def _ssd_in(x, mods, nw, p, *, tr):
    nb, seq, d = x.shape
    halo = F32_SUBLANES
    prev, nxt = _halo_specs(tr, halo, seq, d)
    consts = [nw, p["w_z"], p["w_xbc"], p["w_dt"], p["conv_w"], p["conv_b"], p["dt_bias"]]
    wz, wx, wdt = p["w_z"].shape[1], p["w_xbc"].shape[1], p["w_dt"].shape[1]
    return pl.pallas_call(
        functools.partial(_ssd_in_kernel, tr=tr, halo=halo, seq=seq, cn=512),
        out_shape=[jax.ShapeDtypeStruct((nb, seq, wz), BF16), jax.ShapeDtypeStruct((nb, seq, wx), BF16),
                   jax.ShapeDtypeStruct((nb, seq, wdt), F32)],
        grid=(nb, seq // tr),
        in_specs=[prev, _row_spec(tr, d), nxt, pl.BlockSpec((1, 6, d), lambda b, i: (b, 0, 0))]
                 + [_const_spec(a.shape) for a in consts],
        out_specs=[_row_spec(tr, wz), _row_spec(tr, wx), _row_spec(tr, wdt)],
        compiler_params=_cparams(2),
        name="ssd_in",
    )(x, x, x, mods, *consts)


def _split3(x):
    hi = x.astype(BF16)
    r1 = x - hi.astype(F32)
    mid = r1.astype(BF16)
    lo = (r1 - mid.astype(F32)).astype(BF16)
    return hi, mid, lo


def _ssd_scan_kernel(xs_ref, b_ref, c_ref, dt_ref, alog_ref, dskip_ref, h0_ref, y_ref, hout_ref,
                     yacc, st_f, st_b, g_s, gt_s, dtt_s, wt_s, cb_s, bt_s, tot_s, *, nc):
    q = SSD_CHUNK
    gw = xs_ref.shape[2]
    hpg = gw // SSD_HEADDIM
    pair_w = 2 * SSD_HEADDIM
    st_f[...] = h0_ref[0, 0, 0]
    st_b[...] = h0_ref[0, 0, 1]
    yacc[...] = xs_ref[0].astype(F32) * dskip_ref[0]

    ii = lax.broadcasted_iota(jnp.int32, (q, q), 0)
    jj = lax.broadcasted_iota(jnp.int32, (q, q), 1)
    keep = (ii >= jj, jj >= ii)
    tri_incl = jnp.where(ii >= jj, 1.0, 0.0).astype(BF16)
    lane = lax.broadcasted_iota(jnp.int32, (1, LANES), 1)
    fwd_lane = lane < hpg
    a_neg = jnp.where(lane < 2 * hpg, -jnp.exp(alog_ref[0]), 0.0)
    even_head = (lax.broadcasted_iota(jnp.int32, (q, gw), 1) % pair_w) < SSD_HEADDIM
    even_lane = lax.broadcasted_iota(jnp.int32, (F32_SUBLANES, pair_w), 1) < SSD_HEADDIM

    def prep(ci, carry):
        rows = pl.ds(pl.multiple_of(ci * q, q), q)
        bc, cc = b_ref[0, rows, :], c_ref[0, rows, :]
        dt = dt_ref[0, rows, :]
        a = dt * a_neg
        cs = functools.reduce(jnp.add, [_dot(tri_incl, part) for part in _split3(a)])
        tot = cs[q - 1:q, :]
        g = jnp.where(fwd_lane, cs, a - cs)
        w_state = dt * jnp.exp(jnp.where(fwd_lane, tot - cs, cs - a))
        state_off = jnp.where(fwd_lane, 0.0, -tot)
        g = g * LOG2_E
        g_s[ci] = g
        gt_s[ci] = jnp.concatenate([g.T, jnp.broadcast_to(state_off * LOG2_E, (q, LANES)).T], axis=1)
        dtt_s[ci] = dt.T
        wt_s[ci] = w_state.T
        cb_s[ci] = _dot_nt(cc, bc)
        bt_s[ci] = bc.astype(F32).T
        tot_s[ci] = jnp.broadcast_to(tot, (F32_SUBLANES, LANES))
        return carry

    lax.fori_loop(0, nc, prep, 0, unroll=2)

    def chunk(ci, d, st):
        rows = pl.ds(pl.multiple_of(ci * q, q), q)
        x = xs_ref[0, rows, :]
        x_sel = (jnp.where(even_head, x, jnp.zeros_like(x)), jnp.where(even_head, jnp.zeros_like(x), x))
        cc = c_ref[0, rows, :].astype(F32)
        g, cb, b_t, tot8 = g_s[ci], cb_s[ci], bt_s[ci], tot_s[ci]
        s_val = st[...]
        s_sel = (jnp.where(even_head, s_val, 0.0).astype(BF16), jnp.where(even_head, 0.0, s_val).astype(BF16))
        y_pairs, s_upd, decay = [], [], []
        for pr in range(hpg // 2):
            blk = slice(pr * pair_w, (pr + 1) * pair_w)
            acc, b_scaled, tots = None, [], []
            for e in range(2):
                k = d * hpg + 2 * pr + e
                gi = jnp.broadcast_to(g[:, k:k + 1], (q, q))
                ex_pos = jnp.exp2(jnp.where(keep[d], gi - gt_s[ci, k:k + 1, :q], -jnp.inf))
                ex_state = jnp.exp2(gi - gt_s[ci, k:k + 1, q:])
                lhs = jnp.concatenate([(cb * dtt_s[ci, k:k + 1, :] * ex_pos).astype(BF16),
                                       (cc * ex_state).astype(BF16)], axis=1)
                rhs = jnp.concatenate([x_sel[e][:, blk], s_sel[e][:, blk]], axis=0)
                part = _dot(lhs, rhs)
                acc = part if acc is None else acc + part
                b_scaled.append((b_t * wt_s[ci, k:k + 1, :]).astype(BF16))
                tots.append(jnp.broadcast_to(tot8[:, k:k + 1], (F32_SUBLANES, pair_w)))
            y_pairs.append(acc)
            s_upd.append(_dot(jnp.concatenate(b_scaled, axis=1),
                              jnp.concatenate([x_sel[0][:, blk], x_sel[1][:, blk]], axis=0)))
            decay.append(jnp.where(even_lane, tots[0], tots[1]))
        yacc[rows, :] += jnp.concatenate(y_pairs, axis=1)
        chunk_decay = jnp.exp(jnp.concatenate(decay, axis=1))[0:1, :]
        st[...] = chunk_decay * s_val + jnp.concatenate(s_upd, axis=1)

    def body(ci, carry):
        chunk(ci, 0, st_f)
        chunk(nc - 1 - ci, 1, st_b)
        return carry

    lax.fori_loop(0, nc, body, 0)
    y_ref[0] = yacc[...].astype(BF16)
    hout_ref[0, 0, 0] = st_f[...]
    hout_ref[0, 0, 1] = st_b[...]


def _ssd_scan(xbc, dt, h0, p):
    nb, seq, _ = xbc.shape
    gw = h0.shape[4]
    n_x = SSD_GROUPS * gw // SSD_STATE
    st_spec = pl.BlockSpec((1, 1, 2, SSD_STATE, gw), lambda b, g: (b, g, 0, 0, 0))
    q, nc = SSD_CHUNK, seq // SSD_CHUNK
    return pl.pallas_call(
        functools.partial(_ssd_scan_kernel, nc=nc),
        out_shape=[jax.ShapeDtypeStruct((nb, seq, SSD_GROUPS * gw), BF16), jax.ShapeDtypeStruct(h0.shape, F32)],
        grid=(nb, SSD_GROUPS),
        in_specs=[pl.BlockSpec((1, seq, gw), lambda b, g: (b, 0, g)),
                  pl.BlockSpec((1, seq, SSD_STATE), lambda b, g: (b, 0, n_x + g)),
                  pl.BlockSpec((1, seq, SSD_STATE), lambda b, g: (b, 0, n_x + SSD_GROUPS + g)),
                  pl.BlockSpec((1, seq, LANES), lambda b, g: (b, 0, g)),
                  pl.BlockSpec((1, 1, LANES), lambda b, g: (g, 0, 0)),
                  pl.BlockSpec((1, 1, gw), lambda b, g: (g, 0, 0)),
                  st_spec],
        out_specs=[pl.BlockSpec((1, seq, gw), lambda b, g: (b, 0, g)), st_spec],
        scratch_shapes=[pltpu.VMEM((seq, gw), F32), pltpu.VMEM((SSD_STATE, gw), F32),
                        pltpu.VMEM((SSD_STATE, gw), F32)]
                       + [pltpu.VMEM((nc, rows, width), F32) for rows, width in
                          ((q, LANES), (LANES, 2 * q), (LANES, q), (LANES, q), (q, q), (SSD_STATE, q),
                           (F32_SUBLANES, LANES))],
        compiler_params=_cparams(2),
        name="ssd_scan",
    )(xbc, xbc, xbc, dt, p["a_log"], p["d_skip"], h0)


def _ssd_out_kernel(y_ref, z_ref, x_ref, mod_ref, nw_ref, sn_ref, wo_ref, o_ref):
    y = y_ref[0].astype(F32) * _silu(z_ref[0].astype(F32))
    m = _dot(_rms(y, sn_ref[...]).astype(BF16), wo_ref[...])
    o_ref[0] = x_ref[0] + mod_ref[0, 2:3, :] * _rms(m, nw_ref[1:2, :])


def _ssd_out(y, z, x, mods, nw, p, *, tr):
    nb, seq, d = x.shape
    consts = [nw, p["norm"], p["w_out"]]
    return pl.pallas_call(
        _ssd_out_kernel,
        out_shape=jax.ShapeDtypeStruct((nb, seq, d), F32),
        grid=(nb, seq // tr),
        in_specs=[_row_spec(tr, y.shape[2]), _row_spec(tr, z.shape[2]), _row_spec(tr, d),
                  pl.BlockSpec((1, 6, d), lambda b, i: (b, 0, 0))] + [_const_spec(a.shape) for a in consts],
        out_specs=_row_spec(tr, d),
        compiler_params=_cparams(2),
        name="ssd_out",
    )(y, z, x, mods, *consts)


def _prep_hyb(w_in, q_norm, kv_norm, w_uq, w_ukv, sconv_w, w_out):
    d = w_in.shape[0]
    o_kr = MLA_Q_RANK + MLA_KV_RANK
    kr_block = jnp.pad(w_in[:, o_kr:o_kr + MLA_ROPE], ((0, 0), (MLA_NOPE, HEAD_PAD - MLA_NOPE - MLA_ROPE)))
    w_in_p = jnp.concatenate([w_in[:, :o_kr], kr_block, w_in[:, o_kr + MLA_ROPE:]], axis=1)
    pad_heads = lambda w, width: jnp.pad(w, ((0, 0), (0, 0), (0, HEAD_PAD - width))).reshape(w.shape[0], -1)
    qk = MLA_NOPE + MLA_ROPE
    w_uq_p = pad_heads(w_uq.reshape(MLA_Q_RANK, MLA_HEADS, qk), qk)
    w_ukv_h = w_ukv.reshape(MLA_KV_RANK, MLA_HEADS, MLA_NOPE + MLA_V)
    w_k = pad_heads(w_ukv_h[..., :MLA_NOPE], MLA_NOPE)
    w_v = pad_heads(w_ukv_h[..., MLA_NOPE:], MLA_V)
    n_attn = MLA_HEADS * MLA_V
    w_oa = jnp.pad(w_out[:n_attn].reshape(MLA_HEADS, MLA_V, d), ((0, 0), (0, HEAD_PAD - MLA_V), (0, 0)))
    return dict(w_in=w_in_p.astype(BF16), q_norm=q_norm[None, :], kv_norm=kv_norm[None, :],
                w_uq=w_uq_p.astype(BF16), w_k=w_k.astype(BF16), w_vt=w_v.T.astype(BF16),
                sconv_w=sconv_w, w_oa=w_oa.reshape(MLA_HEADS * HEAD_PAD, d).astype(BF16),
                w_os=w_out[n_attn:].astype(BF16))


def _prep_ssd(w_in, conv_w, conv_b, a_log, dt_bias, d_skip, norm_w, w_out):
    d = w_in.shape[0]
    inner = norm_w.shape[0]
    heads = a_log.shape[1]
    hpg = heads // SSD_GROUPS
    n_xbc = conv_w.shape[1]
    per_group = lambda t: jnp.pad(t.reshape(-1, 2, SSD_GROUPS, hpg).transpose(0, 2, 1, 3).reshape(-1, SSD_GROUPS, 2 * hpg),
                                  ((0, 0), (0, 0), (0, LANES - 2 * hpg)))
    w_dt = per_group(w_in[:, inner + n_xbc:]).reshape(d, SSD_GROUPS * LANES)
    return dict(w_z=w_in[:, :inner].astype(BF16), w_xbc=w_in[:, inner:inner + n_xbc].astype(BF16),
                w_dt=w_dt.astype(BF16), conv_w=conv_w, conv_b=conv_b[None, :],
                dt_bias=per_group(dt_bias.reshape(1, -1)).reshape(1, SSD_GROUPS * LANES),
                a_log=per_group(a_log.reshape(1, -1))[0][:, None, :],
                d_skip=jnp.repeat(d_skip.reshape(SSD_GROUPS, 1, hpg), SSD_HEADDIM, axis=2),
                norm=norm_w[None, :], w_out=w_out.astype(BF16))


def _rope_tables(rows):
    row = jnp.repeat(jnp.arange(rows, dtype=F32), GRID_W)
    col = jnp.tile(jnp.arange(GRID_W, dtype=F32), rows)
    nf = MLA_ROPE // 4
    inv = ROPE_THETA ** (-jnp.arange(nf, dtype=F32) / nf)
    ang = jnp.concatenate([row[:, None] * inv, col[:, None] * inv], axis=-1)
    cos, sin = jnp.cos(ang), jnp.sin(ang)
    n = ang.shape[0]
    half = MLA_ROPE // 2
    ones, zeros = jnp.ones((n, MLA_NOPE), F32), jnp.zeros((n, MLA_NOPE), F32)
    tail1, tail0 = jnp.ones((n, HEAD_PAD - MLA_NOPE - MLA_ROPE), F32), jnp.zeros((n, HEAD_PAD - MLA_NOPE - MLA_ROPE), F32)
    zh = jnp.zeros((n, half), F32)
    return (jnp.concatenate([ones, cos, cos, tail1], axis=1),
            jnp.concatenate([zeros, -sin, zh, tail0], axis=1),
            jnp.concatenate([zeros, zh, sin, tail0], axis=1))


def _identity_tables(n):
    return jnp.ones((n, HEAD_PAD), F32), jnp.zeros((n, HEAD_PAD), F32), jnp.zeros((n, HEAD_PAD), F32)


def _row_tile(seq):
    return min(seq, 512)


def kernel(x, c, ctx, c_ctx, mod_w, mod_b, norm_w, ffn_w_up, ffn_conv_w, ffn_w_down, hyb_w_in, mla_q_norm, mla_kv_norm, mla_w_uq, mla_w_ukv, sconv_w, hyb_w_out, ssd_w_in, ssd_conv_w, ssd_conv_b, ssd_a_log, ssd_dt_bias, ssd_d, ssd_norm, ssd_w_out):
    nb, seq, d = x.shape
    n_ctx = ctx.shape[1]
    depth = mod_w.shape[0]
    tr_l, tr_c = _row_tile(seq), _row_tile(n_ctx)
    tq_l, tq_c = min(seq, 2048), min(n_ctx, 256)

    n_rows = -(-(nb + 1) // F32_SUBLANES) * F32_SUBLANES
    cond = jnp.concatenate([c, c_ctx[None, :], jnp.zeros((n_rows - nb - 1, d), F32)], axis=0)
    mods = _modulation(cond, mod_w, mod_b).reshape(depth, n_rows, 6, d)

    tabs_l = _rope_tables(seq // GRID_W)
    tabs_c = _identity_tables(n_ctx)
    heads = ssd_a_log.shape[2]
    h0 = jnp.zeros((nb, SSD_GROUPS, 2, SSD_STATE, heads // SSD_GROUPS * SSD_HEADDIM), F32)

    for l in range(depth):
        last = l == depth - 1
        i = l // 2
        mod_l = mods[l, :nb]
        mod_c = jnp.broadcast_to(mods[l, nb:nb + 1], (nb, 6, d))
        nw = norm_w[l]
        if l % 2 == 0:
            p = _prep_hyb(hyb_w_in[i], mla_q_norm[i], mla_kv_norm[i], mla_w_uq[i], mla_w_ukv[i], sconv_w[i],
                          hyb_w_out[i])
            q_l, k_l, v_l, gb_l, u_l = _hyb_in(x, mod_l, nw, p, tabs_l, tr=tr_l)
            q_c, k_c, v_c, gb_c, u_c = _hyb_in(ctx, mod_c, nw, p, tabs_c, tr=tr_c)
            attn_l = _attention(q_l, [(k_c, v_c), (k_l, v_l)], tq=tq_l)
            x = _hyb_out(attn_l, gb_l, u_l, x, mod_l, nw, p, tr=tr_l)
            if not last:
                attn_c = _attention(q_c, [(k_c, v_c)], tq=tq_c)
                ctx = _hyb_out(attn_c, gb_c, u_c, ctx, mod_c, nw, p, tr=tr_c)
        else:
            p = _prep_ssd(ssd_w_in[i], ssd_conv_w[i], ssd_conv_b[i], ssd_a_log[i], ssd_dt_bias[i], ssd_d[i],
                          ssd_norm[i], ssd_w_out[i])
            z_c, xbc_c, dt_c = _ssd_in(ctx, mod_c, nw, p, tr=tr_c)
            y_c, h_fin = _ssd_scan(xbc_c, dt_c, h0, p)
            z_l, xbc_l, dt_l = _ssd_in(x, mod_l, nw, p, tr=tr_l)
            y_l, _ = _ssd_scan(xbc_l, dt_l, h_fin, p)
            x = _ssd_out(y_l, z_l, x, mod_l, nw, p, tr=tr_l)
            if not last:
                ctx = _ssd_out(y_c, z_c, ctx, mod_c, nw, p, tr=tr_c)
        wup, cw, wdn = ffn_w_up[l].astype(BF16), ffn_conv_w[l], ffn_w_down[l].astype(BF16)
        x = _ffn(x, mod_l, nw, wup, cw, wdn, tr=tr_l, ck=FFN_CHUNK)
        if not last:
            ctx = _ffn(ctx, mod_c, nw, wup, cw, wdn, tr=tr_c, ck=FFN_CHUNK)
    return x
```

```python
import functools

import jax
import jax.numpy as jnp
from jax import lax
from jax.experimental import pallas as pl
from jax.experimental.pallas import tpu as pltpu

F32 = jnp.float32
BF16 = jnp.bfloat16

EPS = 1e-6
LOG2_E = 1.4426950408889634
GRID_W = 64
ROPE_THETA = 10000.0
MLA_HEADS = 8
MLA_Q_RANK = 384
MLA_KV_RANK = 256
MLA_NOPE = 64
MLA_ROPE = 32
MLA_V = 64
MLA_SCALE = (MLA_NOPE + MLA_ROPE) ** -0.5
HEAD_PAD = 128
SC_WIDTH = 512
SSD_HEADDIM = 64
SSD_GROUPS = 4
SSD_STATE = 128
SSD_CHUNK = 128
FFN_CHUNK = 256
SSD_IN_BLOCK = 512
LANES = 128
F32_SUBLANES = 8
BF16_SUBLANES = 16
VMEM_LIMIT = 56 * 1024 * 1024


def _cparams(n_grid):
    return pltpu.CompilerParams(dimension_semantics=("arbitrary",) * n_grid, vmem_limit_bytes=VMEM_LIMIT)


def _rms(x, w):
    return x * lax.rsqrt(jnp.mean(x * x, axis=-1, keepdims=True) + EPS) * w


def _silu(x):
    return x * jax.nn.sigmoid(x)


def _dot(a, b):
    return jnp.dot(a, b, preferred_element_type=F32)


def _dot_nt(a, b):
    return lax.dot_general(a, b, (((1,), (1,)), ((), ())), preferred_element_type=F32)


def _const_spec(shape):
    nd = len(shape)
    return pl.BlockSpec(shape, lambda *_: (0,) * nd, pipeline_mode=pl.Buffered(1))


def _row_spec(tr, width):
    return pl.BlockSpec((1, tr, width), lambda b, i: (b, i, 0))


def _halo_specs(tr, halo, seq, width):
    per, last = tr // halo, seq // halo - 1
    prev = pl.BlockSpec((1, halo, width), lambda b, i: (b, jnp.maximum(i * per - 1, 0), 0))
    nxt = pl.BlockSpec((1, halo, width), lambda b, i: (b, jnp.minimum((i + 1) * per, last), 0))
    return prev, nxt


def _valid_rows(tr, halo, seq):
    rid = lax.broadcasted_iota(jnp.int32, (tr + 2 * halo, 1), 0) + (pl.program_id(1) * tr - halo)
    return (rid >= 0) & (rid < seq)


def _conv3(u, w):
    rows = u.shape[0]
    return pltpu.roll(u, 1, 0) * w[0:1] + u * w[1:2] + pltpu.roll(u, rows - 1, 0) * w[2:3]


def _mod_kernel(cond_ref, w_ref, b_ref, o_ref):
    cond = _silu(cond_ref[...])
    o_ref[0] = jnp.dot(cond, w_ref[0], preferred_element_type=F32, precision=lax.Precision.HIGHEST) + b_ref[0]


def _modulation(cond, mod_w, mod_b):
    depth, d, n = mod_w.shape
    rows = cond.shape[0]
    tn = 1536
    return pl.pallas_call(
        _mod_kernel,
        out_shape=jax.ShapeDtypeStruct((depth, rows, n), F32),
        grid=(depth, n // tn),
        in_specs=[pl.BlockSpec((rows, d), lambda l, j: (0, 0)),
                  pl.BlockSpec((1, d, tn), lambda l, j: (l, 0, j)),
                  pl.BlockSpec((1, 1, tn), lambda l, j: (l, 0, j))],
        out_specs=pl.BlockSpec((1, rows, tn), lambda l, j: (l, 0, j)),
        compiler_params=_cparams(2),
        name="modulation",
    )(cond, mod_w, mod_b.reshape(depth, 1, n))


def _ffn_kernel(xp_ref, x_ref, xn_ref, mod_ref, nw_ref, wup_ref, cw_ref, wdn_ref, o_ref, h_scr, u_scr, acc_scr,
                *, tr, halo, seq, nchunk, ck):
    x = x_ref[0]
    x_ext = jnp.concatenate([xp_ref[0], x, xn_ref[0]], axis=0)
    sh2, sc2, g2 = mod_ref[0, 3:4, :], mod_ref[0, 4:5, :], mod_ref[0, 5:6, :]
    h = _rms(x_ext, nw_ref[2:3, :]) * (1.0 + sc2) + sh2
    h_scr[...] = jnp.where(_valid_rows(tr, halo, seq), h, 0.0).astype(BF16)

    dff = wdn_ref.shape[0]

    def up(c, slot):
        for half in range(2):
            cols = slice(half * dff + c * ck, half * dff + (c + 1) * ck)
            u_scr[slot, half] = _dot(h_scr[...], wup_ref[:, cols])

    def down(c, slot, first=False):
        ys = []
        for half in range(2):
            cols = slice(half * dff + c * ck, half * dff + (c + 1) * ck)
            ys.append(_conv3(u_scr[slot, half], cw_ref[:, cols])[halo:halo + tr])
        a = _silu(ys[0]) * ys[1]
        part = _dot(a.astype(BF16), wdn_ref[c * ck:(c + 1) * ck, :])
        acc_scr[...] = part if first else acc_scr[...] + part

    up(0, 0)
    for c in range(nchunk):
        if c + 1 < nchunk:
            up(c + 1, (c + 1) % 2)
        down(c, c % 2, first=c == 0)
    o_ref[0] = x + g2 * _rms(acc_scr[...], nw_ref[3:4, :])


def _ffn(x, mods, nw, wup, cw, wdn, *, tr, ck):
    nb, seq, d = x.shape
    halo = F32_SUBLANES
    nchunk = wdn.shape[0] // ck
    prev, nxt = _halo_specs(tr, halo, seq, d)
    kern = functools.partial(_ffn_kernel, tr=tr, halo=halo, seq=seq, nchunk=nchunk, ck=ck)
    return pl.pallas_call(
        kern,
        out_shape=jax.ShapeDtypeStruct((nb, seq, d), F32),
        grid=(nb, seq // tr),
        in_specs=[prev, _row_spec(tr, d), nxt,
                  pl.BlockSpec((1, 6, d), lambda b, i: (b, 0, 0)),
                  _const_spec(nw.shape), _const_spec(wup.shape), _const_spec(cw.shape), _const_spec(wdn.shape)],
        out_specs=_row_spec(tr, d),
        scratch_shapes=[pltpu.VMEM((tr + 2 * halo, d), BF16), pltpu.VMEM((2, 2, tr + 2 * halo, ck), F32),
                        pltpu.VMEM((tr, d), F32)],
        compiler_params=_cparams(2),
        name="ffn",
    )(x, x, x, mods, nw, wup, cw, wdn)


def _rope(x, c, sa, sb):
    w = x.shape[1]
    return x * c + pltpu.roll(x, w - MLA_ROPE // 2, 1) * sa + pltpu.roll(x, MLA_ROPE // 2, 1) * sb


def _hyb_in_kernel(x_ref, mod_ref, nw_ref, win_ref, qn_ref, kvn_ref, wuq_ref, wk_ref, wvt_ref,
                   rc_ref, rsa_ref, rsb_ref, q_ref, k_ref, vt_ref, gb_ref, u_ref):
    sh1, sc1 = mod_ref[0, 0:1, :], mod_ref[0, 1:2, :]
    h = (_rms(x_ref[0], nw_ref[0:1, :]) * (1.0 + sc1) + sh1).astype(BF16)
    o_kv, o_kr, o_gb = MLA_Q_RANK, MLA_Q_RANK + MLA_KV_RANK, MLA_Q_RANK + MLA_KV_RANK + HEAD_PAD
    o_gc, o_xv = o_gb + SC_WIDTH, o_gb + 2 * SC_WIDTH
    rc, rsa, rsb = rc_ref[...], rsa_ref[...], rsb_ref[...]
    tile = lambda t: jnp.concatenate([t] * MLA_HEADS, axis=1)

    cq = _dot(h, win_ref[:, 0:o_kv])
    qn = _rms(cq, qn_ref[...]).astype(BF16)
    q = _rope(_dot(qn, wuq_ref[...]), tile(rc), tile(rsa), tile(rsb))
    q_ref[0] = (q * (MLA_SCALE * LOG2_E)).astype(BF16)

    ckv = _dot(h, win_ref[:, o_kv:o_kr])
    kvn = _rms(ckv, kvn_ref[...]).astype(BF16)
    kr = _rope(_dot(h, win_ref[:, o_kr:o_gb]), rc, rsa, rsb)
    k_ref[0] = (_dot(kvn, wk_ref[...]) + tile(kr)).astype(BF16)
    vt = _dot_nt(wvt_ref[...], kvn)
    row = lax.broadcasted_iota(jnp.int32, vt.shape, 0)
    vt_ref[0] = jnp.where(row % HEAD_PAD == MLA_V, 1.0, vt).astype(BF16)

    gb_ref[0] = _dot(h, win_ref[:, o_gb:o_gc]).astype(BF16)
    gc = _dot(h, win_ref[:, o_gc:o_xv])
    xv = _dot(h, win_ref[:, o_xv:o_xv + SC_WIDTH])
    u_ref[0] = (gc * xv).astype(BF16)


def _hyb_in(x, mods, nw, p, rope_tabs, *, tr):
    nb, seq, d = x.shape
    hp = MLA_HEADS * HEAD_PAD
    consts = [nw, p["w_in"], p["q_norm"], p["kv_norm"], p["w_uq"], p["w_k"], p["w_vt"]]
    tab_spec = pl.BlockSpec((tr, HEAD_PAD), lambda b, i: (i, 0))
    row_out = lambda w: (jax.ShapeDtypeStruct((nb, seq, w), BF16), _row_spec(tr, w))
    outs = [row_out(hp), row_out(hp),
            (jax.ShapeDtypeStruct((nb, hp, seq), BF16), pl.BlockSpec((1, hp, tr), lambda b, i: (b, 0, i))),
            row_out(SC_WIDTH), row_out(SC_WIDTH)]
    return pl.pallas_call(
        _hyb_in_kernel,
        out_shape=[o[0] for o in outs],
        grid=(nb, seq // tr),
        in_specs=[_row_spec(tr, d), pl.BlockSpec((1, 6, d), lambda b, i: (b, 0, 0))]
                 + [_const_spec(a.shape) for a in consts] + [tab_spec] * 3,
        out_specs=[o[1] for o in outs],
        compiler_params=_cparams(2),
        name="hyb_in",
    )(x, mods, *consts, *rope_tabs)


def _attn_kernel(*refs, n_src, n_sub, n_heads):
    q_ref, o_ref = refs[0], refs[1 + 2 * n_src]
    sub = q_ref.shape[1] // n_sub
    tiles = [(hd, t) for hd in range(n_heads) for t in range(n_sub)]

    def qk(hd, t):
        q = q_ref[0, t * sub:(t + 1) * sub, hd * HEAD_PAD:(hd + 1) * HEAD_PAD]
        return [_dot_nt(refs[1 + 2 * s][0, :, hd * HEAD_PAD:(hd + 1) * HEAD_PAD], q) for s in range(n_src)]

    nxt = qk(*tiles[0])
    for i, (hd, t) in enumerate(tiles):
        scores, nxt = nxt, (qk(*tiles[i + 1]) if i + 1 < len(tiles) else None)
        m = functools.reduce(jnp.maximum, [jnp.max(s, axis=0, keepdims=True) for s in scores])
        o_t = None
        for s in range(n_src):
            v_t = refs[2 + 2 * s][0, hd * HEAD_PAD:(hd + 1) * HEAD_PAD, :]
            pv = _dot(v_t, jnp.exp2(scores[s] - m).astype(BF16))
            o_t = pv if o_t is None else o_t + pv
        o_t = o_t / o_t[MLA_V:MLA_V + 1, :]
        o_ref[0, t * sub:(t + 1) * sub, hd * HEAD_PAD:(hd + 1) * HEAD_PAD] = o_t.T.astype(BF16)


def _attention(q, kvs, *, tq, n_heads=2):
    nb, seq, hp = q.shape
    width = n_heads * HEAD_PAD
    in_specs = [pl.BlockSpec((1, tq, width), lambda b, h, i: (b, i, h))]
    args = [q]
    for k, vt in kvs:
        in_specs += [pl.BlockSpec((1, k.shape[1], width), lambda b, h, i: (b, 0, h)),
                     pl.BlockSpec((1, width, k.shape[1]), lambda b, h, i: (b, h, 0))]
        args += [k, vt]
    return pl.pallas_call(
        functools.partial(_attn_kernel, n_src=len(kvs), n_sub=max(tq // 512, 1), n_heads=n_heads),
        out_shape=jax.ShapeDtypeStruct((nb, seq, hp), BF16),
        grid=(nb, MLA_HEADS // n_heads, seq // tq),
        in_specs=in_specs,
        out_specs=pl.BlockSpec((1, tq, width), lambda b, h, i: (b, i, h)),
        compiler_params=_cparams(3),
        name="attention",
    )(*args)


def _hyb_out_kernel(attn_ref, gb_ref, up_ref, u_ref, un_ref, x_ref, mod_ref, nw_ref, cw_ref, woa_ref, wos_ref,
                    o_ref, *, tr, halo, seq):
    u_ext = jnp.concatenate([up_ref[0], u_ref[0], un_ref[0]], axis=0).astype(F32)
    u_ext = jnp.where(_valid_rows(tr, halo, seq), u_ext, 0.0)
    sc = gb_ref[0].astype(F32) * _conv3(u_ext, cw_ref[...])[halo:halo + tr]
    m = _dot(attn_ref[0], woa_ref[...]) + _dot(sc.astype(BF16), wos_ref[...])
    o_ref[0] = x_ref[0] + mod_ref[0, 2:3, :] * _rms(m, nw_ref[1:2, :])


def _hyb_out(attn, gb, u, x, mods, nw, p, *, tr):
    nb, seq, d = x.shape
    halo = BF16_SUBLANES
    prev, nxt = _halo_specs(tr, halo, seq, SC_WIDTH)
    consts = [nw, p["sconv_w"], p["w_oa"], p["w_os"]]
    return pl.pallas_call(
        functools.partial(_hyb_out_kernel, tr=tr, halo=halo, seq=seq),
        out_shape=jax.ShapeDtypeStruct((nb, seq, d), F32),
        grid=(nb, seq // tr),
        in_specs=[_row_spec(tr, attn.shape[2]), _row_spec(tr, SC_WIDTH), prev, _row_spec(tr, SC_WIDTH), nxt,
                  _row_spec(tr, d), pl.BlockSpec((1, 6, d), lambda b, i: (b, 0, 0))]
                 + [_const_spec(a.shape) for a in consts],
        out_specs=_row_spec(tr, d),
        compiler_params=_cparams(2),
        name="hyb_out",
    )(attn, gb, u, u, u, x, mods, *consts)


def _ssd_in_kernel(xp_ref, x_ref, xn_ref, mod_ref, nw_ref, wz_ref, wx_ref, wdt_ref, cw_ref, cb_ref, dtb_ref,
                   z_ref, xbc_ref, dt_ref, *, tr, halo, seq, cn):
    x_ext = jnp.concatenate([xp_ref[0], x_ref[0], xn_ref[0]], axis=0)
    sh1, sc1 = mod_ref[0, 0:1, :], mod_ref[0, 1:2, :]
    h = _rms(x_ext, nw_ref[0:1, :]) * (1.0 + sc1) + sh1
    h = jnp.where(_valid_rows(tr, halo, seq), h, 0.0).astype(BF16)
    mid = slice(halo, halo + tr)
    for c in range(wz_ref.shape[1] // cn):
        cols = slice(c * cn, (c + 1) * cn)
        z_ref[0, :, cols] = _dot(h, wz_ref[:, cols])[mid].astype(BF16)
    for c in range(wx_ref.shape[1] // cn):
        cols = slice(c * cn, (c + 1) * cn)
        y = _conv3(_dot(h, wx_ref[:, cols]), cw_ref[:, cols])[mid] + cb_ref[:, cols]
        xbc_ref[0, :, cols] = _silu(y).astype(BF16)
    dt = _dot(h, wdt_ref[...])[mid] + dtb_ref[...]
    dt_ref[0] = jnp.maximum(dt, 0.0) + jnp.log1p(jnp.exp(-jnp.abs(dt)))


def _ssd_in(x, mods, nw, p, *, tr):
    nb, seq, d = x.shape
    halo = F32_SUBLANES
    prev, nxt = _halo_specs(tr, halo, seq, d)
    consts = [nw, p["w_z"], p["w_xbc"], p["w_dt"], p["conv_w"], p["conv_b"], p["dt_bias"]]
    wz, wx, wdt = p["w_z"].shape[1], p["w_xbc"].shape[1], p["w_dt"].shape[1]
    return pl.pallas_call(
        functools.partial(_ssd_in_kernel, tr=tr, halo=halo, seq=seq, cn=SSD_IN_BLOCK),
        out_shape=[jax.ShapeDtypeStruct((nb, seq, wz), BF16), jax.ShapeDtypeStruct((nb, seq, wx), BF16),
                   jax.ShapeDtypeStruct((nb, seq, wdt), F32)],
        grid=(nb, seq // tr),
        in_specs=[prev, _row_spec(tr, d), nxt, pl.BlockSpec((1, 6, d), lambda b, i: (b, 0, 0))]
                 + [_const_spec(a.shape) for a in consts],
        out_specs=[_row_spec(tr, wz), _row_spec(tr, wx), _row_spec(tr, wdt)],
        compiler_params=_cparams(2),
        name="ssd_in",
    )(x, x, x, mods, *consts)


def _split3(x):
    hi = x.astype(BF16)
    r1 = x - hi.astype(F32)
    mid = r1.astype(BF16)
    lo = (r1 - mid.astype(F32)).astype(BF16)
    return hi, mid, lo


def _ssd_scan_kernel(xs_ref, b_ref, c_ref, dt_ref, alog_ref, dskip_ref, h0_ref, y_ref, hout_ref,
                     yacc, st_f, st_b, g_s, gt_s, wt_s, cb_s, bt_s, tot_s, *, nc):
    q = SSD_CHUNK
    gw = xs_ref.shape[2]
    hpg = gw // SSD_HEADDIM
    pair_w = 2 * SSD_HEADDIM
    st_f[...] = h0_ref[0, 0, 0]
    st_b[...] = h0_ref[0, 0, 1]
    yacc[...] = xs_ref[0].astype(F32) * dskip_ref[0]

    ii = lax.broadcasted_iota(jnp.int32, (q, q), 0)
    jj = lax.broadcasted_iota(jnp.int32, (q, q), 1)
    keep = (ii >= jj, jj >= ii)
    tri_incl = jnp.where(ii >= jj, 1.0, 0.0).astype(BF16)
    lane = lax.broadcasted_iota(jnp.int32, (1, LANES), 1)
    fwd_lane = lane < hpg
    a_neg = jnp.where(lane < 2 * hpg, -jnp.exp(alog_ref[0]), 0.0)
    even_pair_lane = lax.broadcasted_iota(jnp.int32, (q, pair_w), 1) < SSD_HEADDIM
    even_lane = lax.broadcasted_iota(jnp.int32, (F32_SUBLANES, pair_w), 1) < SSD_HEADDIM

    def prep(ci, carry):
        rows = pl.ds(pl.multiple_of(ci * q, q), q)
        bc, cc = b_ref[0, rows, :], c_ref[0, rows, :]
        dt = dt_ref[0, rows, :]
        a = dt * a_neg
        cs = functools.reduce(jnp.add, [_dot(tri_incl, part) for part in _split3(a)])
        tot = cs[q - 1:q, :]
        g = jnp.where(fwd_lane, cs, a - cs)
        w_state = dt * jnp.exp(jnp.where(fwd_lane, tot - cs, cs - a))
        state_off = jnp.where(fwd_lane, 0.0, -tot)
        g = g * LOG2_E
        g_s[ci] = g
        gt_s[ci] = jnp.concatenate([(g - jnp.log2(dt)).T,
                                    jnp.broadcast_to(state_off * LOG2_E, (q, LANES)).T], axis=1)
        wt_s[ci] = w_state.T
        cb_s[ci] = _dot_nt(cc, bc)
        bt_s[ci] = bc.astype(F32).T
        tot_s[ci] = jnp.broadcast_to(tot, (F32_SUBLANES, LANES))
        return carry

    lax.fori_loop(0, nc, prep, 0, unroll=2)

    def chunk(ci, d, st):
        rows = pl.ds(pl.multiple_of(ci * q, q), q)
        x = xs_ref[0, rows, :]
        cc = c_ref[0, rows, :].astype(F32)
        g, cb, b_t, tot8 = g_s[ci], cb_s[ci], bt_s[ci], tot_s[ci]
        s_val = st[...]
        s_bf = s_val.astype(BF16)
        y_pairs, s_upd, decay = [], [], []
        for pr in range(hpg // 2):
            blk = slice(pr * pair_w, (pr + 1) * pair_w)
            rhs = jnp.concatenate([x[:, blk], s_bf[:, blk]], axis=0)
            ys, ups, tots = [], [], []
            for e in range(2):
                k = d * hpg + 2 * pr + e
                gi = jnp.broadcast_to(g[:, k:k + 1], (q, q))
                ex_pos = jnp.exp2(jnp.where(keep[d], gi - gt_s[ci, k:k + 1, :q], -jnp.inf))
                ex_state = jnp.exp2(gi - gt_s[ci, k:k + 1, q:])
                lhs = jnp.concatenate([(cb * ex_pos).astype(BF16), (cc * ex_state).astype(BF16)], axis=1)
                ys.append(_dot(lhs, rhs))
                ups.append(_dot((b_t * wt_s[ci, k:k + 1, :]).astype(BF16), x[:, blk]))
                tots.append(jnp.broadcast_to(tot8[:, k:k + 1], (F32_SUBLANES, pair_w)))
            y_pairs.append(jnp.where(even_pair_lane, ys[0], ys[1]))
            s_upd.append(jnp.where(even_pair_lane, ups[0], ups[1]))
            decay.append(jnp.where(even_lane, tots[0], tots[1]))
        yacc[rows, :] += jnp.concatenate(y_pairs, axis=1)
        chunk_decay = jnp.exp(jnp.concatenate(decay, axis=1))[0:1, :]
        st[...] = chunk_decay * s_val + jnp.concatenate(s_upd, axis=1)

    def body(ci, carry):
        chunk(ci, 0, st_f)
        chunk(nc - 1 - ci, 1, st_b)
        return carry

    lax.fori_loop(0, nc, body, 0)
    y_ref[0] = yacc[...].astype(BF16)
    hout_ref[0, 0, 0] = st_f[...]
    hout_ref[0, 0, 1] = st_b[...]


def _ssd_scan(xbc, dt, h0, p):
    nb, seq, _ = xbc.shape
    gw = h0.shape[4]
    n_x = SSD_GROUPS * gw // SSD_STATE
    st_spec = pl.BlockSpec((1, 1, 2, SSD_STATE, gw), lambda b, g: (b, g, 0, 0, 0))
    q, nc = SSD_CHUNK, seq // SSD_CHUNK
    return pl.pallas_call(
        functools.partial(_ssd_scan_kernel, nc=nc),
        out_shape=[jax.ShapeDtypeStruct((nb, seq, SSD_GROUPS * gw), BF16), jax.ShapeDtypeStruct(h0.shape, F32)],
        grid=(nb, SSD_GROUPS),
        in_specs=[pl.BlockSpec((1, seq, gw), lambda b, g: (b, 0, g)),
                  pl.BlockSpec((1, seq, SSD_STATE), lambda b, g: (b, 0, n_x + g)),
                  pl.BlockSpec((1, seq, SSD_STATE), lambda b, g: (b, 0, n_x + SSD_GROUPS + g)),
                  pl.BlockSpec((1, seq, LANES), lambda b, g: (b, 0, g)),
                  pl.BlockSpec((1, 1, LANES), lambda b, g: (g, 0, 0)),
                  pl.BlockSpec((1, 1, gw), lambda b, g: (g, 0, 0)),
                  st_spec],
        out_specs=[pl.BlockSpec((1, seq, gw), lambda b, g: (b, 0, g)), st_spec],
        scratch_shapes=[pltpu.VMEM((seq, gw), F32), pltpu.VMEM((SSD_STATE, gw), F32),
                        pltpu.VMEM((SSD_STATE, gw), F32)]
                       + [pltpu.VMEM((nc, rows, width), F32) for rows, width in
                          ((q, LANES), (LANES, 2 * q), (LANES, q), (q, q), (SSD_STATE, q),
                           (F32_SUBLANES, LANES))],
        compiler_params=_cparams(2),
        name="ssd_scan",
    )(xbc, xbc, xbc, dt, p["a_log"], p["d_skip"], h0)


def _ssd_out_kernel(y_ref, z_ref, x_ref, mod_ref, nw_ref, sn_ref, wo_ref, o_ref):
    y = y_ref[0].astype(F32) * _silu(z_ref[0].astype(F32))
    m = _dot(_rms(y, sn_ref[...]).astype(BF16), wo_ref[...])
    o_ref[0] = x_ref[0] + mod_ref[0, 2:3, :] * _rms(m, nw_ref[1:2, :])


def _ssd_out(y, z, x, mods, nw, p, *, tr):
    nb, seq, d = x.shape
    consts = [nw, p["norm"], p["w_out"]]
    return pl.pallas_call(
        _ssd_out_kernel,
        out_shape=jax.ShapeDtypeStruct((nb, seq, d), F32),
        grid=(nb, seq // tr),
        in_specs=[_row_spec(tr, y.shape[2]), _row_spec(tr, z.shape[2]), _row_spec(tr, d),
                  pl.BlockSpec((1, 6, d), lambda b, i: (b, 0, 0))] + [_const_spec(a.shape) for a in consts],
        out_specs=_row_spec(tr, d),
        compiler_params=_cparams(2),
        name="ssd_out",
    )(y, z, x, mods, *consts)


def _prep_hyb(w_in, q_norm, kv_norm, w_uq, w_ukv, sconv_w, w_out):
    d = w_in.shape[0]
    o_kr = MLA_Q_RANK + MLA_KV_RANK
    kr_block = jnp.pad(w_in[:, o_kr:o_kr + MLA_ROPE], ((0, 0), (MLA_NOPE, HEAD_PAD - MLA_NOPE - MLA_ROPE)))
    w_in_p = jnp.concatenate([w_in[:, :o_kr], kr_block, w_in[:, o_kr + MLA_ROPE:]], axis=1)
    pad_heads = lambda w, width: jnp.pad(w, ((0, 0), (0, 0), (0, HEAD_PAD - width))).reshape(w.shape[0], -1)
    qk = MLA_NOPE + MLA_ROPE
    w_uq_p = pad_heads(w_uq.reshape(MLA_Q_RANK, MLA_HEADS, qk), qk)
    w_ukv_h = w_ukv.reshape(MLA_KV_RANK, MLA_HEADS, MLA_NOPE + MLA_V)
    w_k = pad_heads(w_ukv_h[..., :MLA_NOPE], MLA_NOPE)
    w_v = pad_heads(w_ukv_h[..., MLA_NOPE:], MLA_V)
    n_attn = MLA_HEADS * MLA_V
    w_oa = jnp.pad(w_out[:n_attn].reshape(MLA_HEADS, MLA_V, d), ((0, 0), (0, HEAD_PAD - MLA_V), (0, 0)))
    return dict(w_in=w_in_p.astype(BF16), q_norm=q_norm[None, :], kv_norm=kv_norm[None, :],
                w_uq=w_uq_p.astype(BF16), w_k=w_k.astype(BF16), w_vt=w_v.T.astype(BF16),
                sconv_w=sconv_w, w_oa=w_oa.reshape(MLA_HEADS * HEAD_PAD, d).astype(BF16),
                w_os=w_out[n_attn:].astype(BF16))


def _prep_ssd(w_in, conv_w, conv_b, a_log, dt_bias, d_skip, norm_w, w_out):
    d = w_in.shape[0]
    inner = norm_w.shape[0]
    heads = a_log.shape[1]
    hpg = heads // SSD_GROUPS
    n_xbc = conv_w.shape[1]
    per_group = lambda t: jnp.pad(t.reshape(-1, 2, SSD_GROUPS, hpg).transpose(0, 2, 1, 3).reshape(-1, SSD_GROUPS, 2 * hpg),
                                  ((0, 0), (0, 0), (0, LANES - 2 * hpg)))
    w_dt = per_group(w_in[:, inner + n_xbc:]).reshape(d, SSD_GROUPS * LANES)
    return dict(w_z=w_in[:, :inner].astype(BF16), w_xbc=w_in[:, inner:inner + n_xbc].astype(BF16),
                w_dt=w_dt.astype(BF16), conv_w=conv_w, conv_b=conv_b[None, :],
                dt_bias=per_group(dt_bias.reshape(1, -1)).reshape(1, SSD_GROUPS * LANES),
                a_log=per_group(a_log.reshape(1, -1))[0][:, None, :],
                d_skip=jnp.repeat(d_skip.reshape(SSD_GROUPS, 1, hpg), SSD_HEADDIM, axis=2),
                norm=norm_w[None, :], w_out=w_out.astype(BF16))


def _rope_tables(rows):
    row = jnp.repeat(jnp.arange(rows, dtype=F32), GRID_W)
    col = jnp.tile(jnp.arange(GRID_W, dtype=F32), rows)
    nf = MLA_ROPE // 4
    inv = ROPE_THETA ** (-jnp.arange(nf, dtype=F32) / nf)
    ang = jnp.concatenate([row[:, None] * inv, col[:, None] * inv], axis=-1)
    cos, sin = jnp.cos(ang), jnp.sin(ang)
    n = ang.shape[0]
    half = MLA_ROPE // 2
    ones, zeros = jnp.ones((n, MLA_NOPE), F32), jnp.zeros((n, MLA_NOPE), F32)
    tail1, tail0 = jnp.ones((n, HEAD_PAD - MLA_NOPE - MLA_ROPE), F32), jnp.zeros((n, HEAD_PAD - MLA_NOPE - MLA_ROPE), F32)
    zh = jnp.zeros((n, half), F32)
    return (jnp.concatenate([ones, cos, cos, tail1], axis=1),
            jnp.concatenate([zeros, -sin, zh, tail0], axis=1),
            jnp.concatenate([zeros, zh, sin, tail0], axis=1))


def _identity_tables(n):
    return jnp.ones((n, HEAD_PAD), F32), jnp.zeros((n, HEAD_PAD), F32), jnp.zeros((n, HEAD_PAD), F32)


def _row_tile(seq):
    return min(seq, 512)


def kernel(x, c, ctx, c_ctx, mod_w, mod_b, norm_w, ffn_w_up, ffn_conv_w, ffn_w_down, hyb_w_in, mla_q_norm, mla_kv_norm, mla_w_uq, mla_w_ukv, sconv_w, hyb_w_out, ssd_w_in, ssd_conv_w, ssd_conv_b, ssd_a_log, ssd_dt_bias, ssd_d, ssd_norm, ssd_w_out):
    nb, seq, d = x.shape
    n_ctx = ctx.shape[1]
    depth = mod_w.shape[0]
    tr_l, tr_c = _row_tile(seq), _row_tile(n_ctx)
    tq_l, tq_c = min(seq, 2048), min(n_ctx, 256)

    n_rows = -(-(nb + 1) // F32_SUBLANES) * F32_SUBLANES
    cond = jnp.concatenate([c, c_ctx[None, :], jnp.zeros((n_rows - nb - 1, d), F32)], axis=0)
    mods = _modulation(cond, mod_w, mod_b).reshape(depth, n_rows, 6, d)

    tabs_l = _rope_tables(seq // GRID_W)
    tabs_c = _identity_tables(n_ctx)
    heads = ssd_a_log.shape[2]
    h0 = jnp.zeros((nb, SSD_GROUPS, 2, SSD_STATE, heads // SSD_GROUPS * SSD_HEADDIM), F32)

    for l in range(depth):
        last = l == depth - 1
        i = l // 2
        mod_l = mods[l, :nb]
        mod_c = jnp.broadcast_to(mods[l, nb:nb + 1], (nb, 6, d))
        nw = norm_w[l]
        if l % 2 == 0:
            p = _prep_hyb(hyb_w_in[i], mla_q_norm[i], mla_kv_norm[i], mla_w_uq[i], mla_w_ukv[i], sconv_w[i],
                          hyb_w_out[i])
            q_l, k_l, v_l, gb_l, u_l = _hyb_in(x, mod_l, nw, p, tabs_l, tr=tr_l)
            q_c, k_c, v_c, gb_c, u_c = _hyb_in(ctx, mod_c, nw, p, tabs_c, tr=tr_c)
            attn_l = _attention(q_l, [(k_c, v_c), (k_l, v_l)], tq=tq_l)
            x = _hyb_out(attn_l, gb_l, u_l, x, mod_l, nw, p, tr=tr_l)
            if not last:
                attn_c = _attention(q_c, [(k_c, v_c)], tq=tq_c)
                ctx = _hyb_out(attn_c, gb_c, u_c, ctx, mod_c, nw, p, tr=tr_c)
        else:
            p = _prep_ssd(ssd_w_in[i], ssd_conv_w[i], ssd_conv_b[i], ssd_a_log[i], ssd_dt_bias[i], ssd_d[i],
                          ssd_norm[i], ssd_w_out[i])
            z_c, xbc_c, dt_c = _ssd_in(ctx, mod_c, nw, p, tr=tr_c)
            y_c, h_fin = _ssd_scan(xbc_c, dt_c, h0, p)
            z_l, xbc_l, dt_l = _ssd_in(x, mod_l, nw, p, tr=tr_l)
            y_l, _ = _ssd_scan(xbc_l, dt_l, h_fin, p)
            x = _ssd_out(y_l, z_l, x, mod_l, nw, p, tr=tr_l)
            if not last:
                ctx = _ssd_out(y_c, z_c, ctx, mod_c, nw, p, tr=tr_c)
        wup, cw, wdn = ffn_w_up[l].astype(BF16), ffn_conv_w[l], ffn_w_down[l].astype(BF16)
        x = _ffn(x, mod_l, nw, wup, cw, wdn, tr=tr_l, ck=FFN_CHUNK)
        if not last:
            ctx = _ffn(ctx, mod_c, nw, wup, cw, wdn, tr=tr_c, ck=FFN_CHUNK)
    return x
```

```python
import functools

import jax
import jax.numpy as jnp
from jax import lax
from jax.experimental import pallas as pl
from jax.experimental.pallas import tpu as pltpu

F32 = jnp.float32
BF16 = jnp.bfloat16

EPS = 1e-6
LOG2_E = 1.4426950408889634
GRID_W = 64
ROPE_THETA = 10000.0
MLA_HEADS = 8
MLA_Q_RANK = 384
MLA_KV_RANK = 256
MLA_NOPE = 64
MLA_ROPE = 32
MLA_V = 64
MLA_SCALE = (MLA_NOPE + MLA_ROPE) ** -0.5
HEAD_PAD = 128
SC_WIDTH = 512
SSD_HEADDIM = 64
SSD_GROUPS = 4
SSD_STATE = 128
SSD_CHUNK = 128
FFN_CHUNK = 256
SSD_IN_BLOCK = 512
LANES = 128
F32_SUBLANES = 8
BF16_SUBLANES = 16
VMEM_LIMIT = 56 * 1024 * 1024


def _cparams(n_grid):
    return pltpu.CompilerParams(dimension_semantics=("arbitrary",) * n_grid, vmem_limit_bytes=VMEM_LIMIT)


def _rms(x, w):
    return x * lax.rsqrt(jnp.mean(x * x, axis=-1, keepdims=True) + EPS) * w


def _silu(x):
    return x * jax.nn.sigmoid(x)


def _dot(a, b):
    return jnp.dot(a, b, preferred_element_type=F32)


def _dot_nt(a, b):
    return lax.dot_general(a, b, (((1,), (1,)), ((), ())), preferred_element_type=F32)


def _const_spec(shape):
    nd = len(shape)
    return pl.BlockSpec(shape, lambda *_: (0,) * nd, pipeline_mode=pl.Buffered(1))


def _row_spec(tr, width):
    return pl.BlockSpec((1, tr, width), lambda b, i: (b, i, 0))


def _halo_specs(tr, halo, seq, width):
    per, last = tr // halo, seq // halo - 1
    prev = pl.BlockSpec((1, halo, width), lambda b, i: (b, jnp.maximum(i * per - 1, 0), 0))
    nxt = pl.BlockSpec((1, halo, width), lambda b, i: (b, jnp.minimum((i + 1) * per, last), 0))
    return prev, nxt


def _valid_rows(tr, halo, seq):
    rid = lax.broadcasted_iota(jnp.int32, (tr + 2 * halo, 1), 0) + (pl.program_id(1) * tr - halo)
    return (rid >= 0) & (rid < seq)


def _conv3(u, w):
    rows = u.shape[0]
    return pltpu.roll(u, 1, 0) * w[0:1] + u * w[1:2] + pltpu.roll(u, rows - 1, 0) * w[2:3]


def _mod_kernel(cond_ref, w_ref, b_ref, o_ref):
    cond = _silu(cond_ref[...])
    o_ref[0] = jnp.dot(cond, w_ref[0], preferred_element_type=F32, precision=lax.Precision.HIGHEST) + b_ref[0]


def _modulation(cond, mod_w, mod_b):
    depth, d, n = mod_w.shape
    rows = cond.shape[0]
    tn = 1536
    return pl.pallas_call(
        _mod_kernel,
        out_shape=jax.ShapeDtypeStruct((depth, rows, n), F32),
        grid=(depth, n // tn),
        in_specs=[pl.BlockSpec((rows, d), lambda l, j: (0, 0)),
                  pl.BlockSpec((1, d, tn), lambda l, j: (l, 0, j)),
                  pl.BlockSpec((1, 1, tn), lambda l, j: (l, 0, j))],
        out_specs=pl.BlockSpec((1, rows, tn), lambda l, j: (l, 0, j)),
        compiler_params=_cparams(2),
        name="modulation",
    )(cond, mod_w, mod_b.reshape(depth, 1, n))


def _ffn_kernel(xp_ref, x_ref, xn_ref, mod_ref, nw_ref, wup_ref, cw_ref, wdn_ref, o_ref, h_scr, u_scr,
                *, tr, halo, seq, nchunk, ck):
    x = x_ref[0]
    x_ext = jnp.concatenate([xp_ref[0], x, xn_ref[0]], axis=0)
    sh2, sc2, g2 = mod_ref[0, 3:4, :], mod_ref[0, 4:5, :], mod_ref[0, 5:6, :]
    h = _rms(x_ext, nw_ref[2:3, :]) * (1.0 + sc2) + sh2
    h_scr[...] = jnp.where(_valid_rows(tr, halo, seq), h, 0.0).astype(BF16)

    dff = wdn_ref.shape[0]

    def up(c, slot):
        for half in range(2):
            cols = slice(half * dff + c * ck, half * dff + (c + 1) * ck)
            u_scr[slot, half] = _dot(h_scr[...], wup_ref[:, cols])

    def gate(c, slot):
        ys = []
        for half in range(2):
            cols = slice(half * dff + c * ck, half * dff + (c + 1) * ck)
            ys.append(_conv3(u_scr[slot, half], cw_ref[:, cols])[halo:halo + tr])
        return (_silu(ys[0]) * ys[1]).astype(BF16)

    up(0, 0)
    gated = []
    for c in range(nchunk):
        if c + 1 < nchunk:
            up(c + 1, (c + 1) % 2)
        gated.append(gate(c, c % 2))
    f = _dot(jnp.concatenate(gated, axis=1), wdn_ref[...])
    o_ref[0] = x + g2 * _rms(f, nw_ref[3:4, :])


def _ffn(x, mods, nw, wup, cw, wdn, *, tr, ck):
    nb, seq, d = x.shape
    halo = F32_SUBLANES
    nchunk = wdn.shape[0] // ck
    prev, nxt = _halo_specs(tr, halo, seq, d)
    kern = functools.partial(_ffn_kernel, tr=tr, halo=halo, seq=seq, nchunk=nchunk, ck=ck)
    return pl.pallas_call(
        kern,
        out_shape=jax.ShapeDtypeStruct((nb, seq, d), F32),
        grid=(nb, seq // tr),
        in_specs=[prev, _row_spec(tr, d), nxt,
                  pl.BlockSpec((1, 6, d), lambda b, i: (b, 0, 0)),
                  _const_spec(nw.shape), _const_spec(wup.shape), _const_spec(cw.shape), _const_spec(wdn.shape)],
        out_specs=_row_spec(tr, d),
        scratch_shapes=[pltpu.VMEM((tr + 2 * halo, d), BF16), pltpu.VMEM((2, 2, tr + 2 * halo, ck), F32)],
        compiler_params=_cparams(2),
        name="ffn",
    )(x, x, x, mods, nw, wup, cw, wdn)


def _rope(x, c, sa, sb):
    w = x.shape[1]
    return x * c + pltpu.roll(x, w - MLA_ROPE // 2, 1) * sa + pltpu.roll(x, MLA_ROPE // 2, 1) * sb


def _hyb_in_kernel(x_ref, mod_ref, nw_ref, win_ref, qn_ref, kvn_ref, wuq_ref, wk_ref, wvt_ref,
                   rc_ref, rsa_ref, rsb_ref, q_ref, k_ref, vt_ref, gb_ref, u_ref):
    sh1, sc1 = mod_ref[0, 0:1, :], mod_ref[0, 1:2, :]
    h = (_rms(x_ref[0], nw_ref[0:1, :]) * (1.0 + sc1) + sh1).astype(BF16)
    o_kv, o_kr, o_gb = MLA_Q_RANK, MLA_Q_RANK + MLA_KV_RANK, MLA_Q_RANK + MLA_KV_RANK + HEAD_PAD
    o_gc, o_xv = o_gb + SC_WIDTH, o_gb + 2 * SC_WIDTH
    rc, rsa, rsb = rc_ref[...], rsa_ref[...], rsb_ref[...]
    tile = lambda t: jnp.concatenate([t] * MLA_HEADS, axis=1)

    cq = _dot(h, win_ref[:, 0:o_kv])
    qn = _rms(cq, qn_ref[...]).astype(BF16)
    q = _rope(_dot(qn, wuq_ref[...]), tile(rc), tile(rsa), tile(rsb))
    q_ref[0] = (q * (MLA_SCALE * LOG2_E)).astype(BF16)

    ckv = _dot(h, win_ref[:, o_kv:o_kr])
    kvn = _rms(ckv, kvn_ref[...]).astype(BF16)
    kr = _rope(_dot(h, win_ref[:, o_kr:o_gb]), rc, rsa, rsb)
    k_ref[0] = (_dot(kvn, wk_ref[...]) + tile(kr)).astype(BF16)
    vt = _dot_nt(wvt_ref[...], kvn)
    row = lax.broadcasted_iota(jnp.int32, vt.shape, 0)
    vt_ref[0] = jnp.where(row % HEAD_PAD == MLA_V, 1.0, vt).astype(BF16)

    gb_ref[0] = _dot(h, win_ref[:, o_gb:o_gc]).astype(BF16)
    gc = _dot(h, win_ref[:, o_gc:o_xv])
    xv = _dot(h, win_ref[:, o_xv:o_xv + SC_WIDTH])
    u_ref[0] = (gc * xv).astype(BF16)


def _hyb_in(x, mods, nw, p, rope_tabs, *, tr):
    nb, seq, d = x.shape
    hp = MLA_HEADS * HEAD_PAD
    consts = [nw, p["w_in"], p["q_norm"], p["kv_norm"], p["w_uq"], p["w_k"], p["w_vt"]]
    tab_spec = pl.BlockSpec((tr, HEAD_PAD), lambda b, i: (i, 0))
    row_out = lambda w: (jax.ShapeDtypeStruct((nb, seq, w), BF16), _row_spec(tr, w))
    outs = [row_out(hp), row_out(hp),
            (jax.ShapeDtypeStruct((nb, hp, seq), BF16), pl.BlockSpec((1, hp, tr), lambda b, i: (b, 0, i))),
            row_out(SC_WIDTH), row_out(SC_WIDTH)]
    return pl.pallas_call(
        _hyb_in_kernel,
        out_shape=[o[0] for o in outs],
        grid=(nb, seq // tr),
        in_specs=[_row_spec(tr, d), pl.BlockSpec((1, 6, d), lambda b, i: (b, 0, 0))]
                 + [_const_spec(a.shape) for a in consts] + [tab_spec] * 3,
        out_specs=[o[1] for o in outs],
        compiler_params=_cparams(2),
        name="hyb_in",
    )(x, mods, *consts, *rope_tabs)


def _attn_kernel(*refs, n_src, n_sub, n_heads):
    q_ref, o_ref = refs[0], refs[1 + 2 * n_src]
    sub = q_ref.shape[1] // n_sub
    tiles = [(hd, t) for hd in range(n_heads) for t in range(n_sub)]

    def qk(hd, t):
        q = q_ref[0, t * sub:(t + 1) * sub, hd * HEAD_PAD:(hd + 1) * HEAD_PAD]
        return [_dot_nt(refs[1 + 2 * s][0, :, hd * HEAD_PAD:(hd + 1) * HEAD_PAD], q) for s in range(n_src)]

    nxt = qk(*tiles[0])
    for i, (hd, t) in enumerate(tiles):
        scores, nxt = nxt, (qk(*tiles[i + 1]) if i + 1 < len(tiles) else None)
        m = functools.reduce(jnp.maximum, [jnp.max(s, axis=0, keepdims=True) for s in scores])
        o_t = None
        for s in range(n_src):
            v_t = refs[2 + 2 * s][0, hd * HEAD_PAD:(hd + 1) * HEAD_PAD, :]
            pv = _dot(v_t, jnp.exp2(scores[s] - m).astype(BF16))
            o_t = pv if o_t is None else o_t + pv
        o_t = o_t / o_t[MLA_V:MLA_V + 1, :]
        o_ref[0, t * sub:(t + 1) * sub, hd * HEAD_PAD:(hd + 1) * HEAD_PAD] = o_t.T.astype(BF16)


def _attention(q, kvs, *, tq, n_heads=2):
    nb, seq, hp = q.shape
    width = n_heads * HEAD_PAD
    in_specs = [pl.BlockSpec((1, tq, width), lambda b, h, i: (b, i, h))]
    args = [q]
    for k, vt in kvs:
        in_specs += [pl.BlockSpec((1, k.shape[1], width), lambda b, h, i: (b, 0, h)),
                     pl.BlockSpec((1, width, k.shape[1]), lambda b, h, i: (b, h, 0))]
        args += [k, vt]
    return pl.pallas_call(
        functools.partial(_attn_kernel, n_src=len(kvs), n_sub=max(tq // 512, 1), n_heads=n_heads),
        out_shape=jax.ShapeDtypeStruct((nb, seq, hp), BF16),
        grid=(nb, MLA_HEADS // n_heads, seq // tq),
        in_specs=in_specs,
        out_specs=pl.BlockSpec((1, tq, width), lambda b, h, i: (b, i, h)),
        compiler_params=_cparams(3),
        name="attention",
    )(*args)


def _hyb_out_kernel(attn_ref, gb_ref, up_ref, u_ref, un_ref, x_ref, mod_ref, nw_ref, cw_ref, woa_ref, wos_ref,
                    o_ref, *, tr, halo, seq):
    u_ext = jnp.concatenate([up_ref[0], u_ref[0], un_ref[0]], axis=0).astype(F32)
    u_ext = jnp.where(_valid_rows(tr, halo, seq), u_ext, 0.0)
    sc = gb_ref[0].astype(F32) * _conv3(u_ext, cw_ref[...])[halo:halo + tr]
    m = _dot(attn_ref[0], woa_ref[...]) + _dot(sc.astype(BF16), wos_ref[...])
    o_ref[0] = x_ref[0] + mod_ref[0, 2:3, :] * _rms(m, nw_ref[1:2, :])


def _hyb_out(attn, gb, u, x, mods, nw, p, *, tr):
    nb, seq, d = x.shape
    halo = BF16_SUBLANES
    prev, nxt = _halo_specs(tr, halo, seq, SC_WIDTH)
    consts = [nw, p["sconv_w"], p["w_oa"], p["w_os"]]
    return pl.pallas_call(
        functools.partial(_hyb_out_kernel, tr=tr, halo=halo, seq=seq),
        out_shape=jax.ShapeDtypeStruct((nb, seq, d), F32),
        grid=(nb, seq // tr),
        in_specs=[_row_spec(tr, attn.shape[2]), _row_spec(tr, SC_WIDTH), prev, _row_spec(tr, SC_WIDTH), nxt,
                  _row_spec(tr, d), pl.BlockSpec((1, 6, d), lambda b, i: (b, 0, 0))]
                 + [_const_spec(a.shape) for a in consts],
        out_specs=_row_spec(tr, d),
        compiler_params=_cparams(2),
        name="hyb_out",
    )(attn, gb, u, u, u, x, mods, *consts)


def _ssd_in_kernel(xp_ref, x_ref, xn_ref, mod_ref, nw_ref, wz_ref, wx_ref, wdt_ref, cw_ref, cb_ref, dtb_ref,
                   z_ref, xbc_ref, dt_ref, h_scr, ua_scr, ub_scr, *, tr, halo, seq, cn):
    x_ext = jnp.concatenate([xp_ref[0], x_ref[0], xn_ref[0]], axis=0)
    sh1, sc1 = mod_ref[0, 0:1, :], mod_ref[0, 1:2, :]
    h = _rms(x_ext, nw_ref[0:1, :]) * (1.0 + sc1) + sh1
    h_scr[...] = jnp.where(_valid_rows(tr, halo, seq), h, 0.0).astype(BF16)
    mid = slice(halo, halo + tr)
    for c in range(wz_ref.shape[1] // cn):
        cols = slice(c * cn, (c + 1) * cn)
        z_ref[0, :, cols] = _dot(h_scr[...], wz_ref[:, cols])[mid].astype(BF16)
    dyn0 = jnp.minimum(pl.program_id(1), 0)
    bufs = (ua_scr, ub_scr)
    n_x = wx_ref.shape[1] // cn
    bufs[0][dyn0] = _dot(h_scr[...], wx_ref[:, 0:cn])
    for c in range(n_x):
        cols = slice(c * cn, (c + 1) * cn)
        if c + 1 < n_x:
            bufs[(c + 1) % 2][dyn0] = _dot(h_scr[...], wx_ref[:, (c + 1) * cn:(c + 2) * cn])
        y = _conv3(bufs[c % 2][dyn0], cw_ref[:, cols])[mid] + cb_ref[:, cols]
        xbc_ref[0, :, cols] = _silu(y).astype(BF16)
    dt = _dot(h_scr[...], wdt_ref[...])[mid] + dtb_ref[...]
    dt_ref[0] = jnp.maximum(dt, 0.0) + jnp.log1p(jnp.exp(-jnp.abs(dt)))


def _ssd_in(x, mods, nw, p, *, tr):
    nb, seq, d = x.shape
    halo = F32_SUBLANES
    prev, nxt = _halo_specs(tr, halo, seq, d)
    consts = [nw, p["w_z"], p["w_xbc"], p["w_dt"], p["conv_w"], p["conv_b"], p["dt_bias"]]
    wz, wx, wdt = p["w_z"].shape[1], p["w_xbc"].shape[1], p["w_dt"].shape[1]
    return pl.pallas_call(
        functools.partial(_ssd_in_kernel, tr=tr, halo=halo, seq=seq, cn=SSD_IN_BLOCK),
        out_shape=[jax.ShapeDtypeStruct((nb, seq, wz), BF16), jax.ShapeDtypeStruct((nb, seq, wx), BF16),
                   jax.ShapeDtypeStruct((nb, seq, wdt), F32)],
        grid=(nb, seq // tr),
        in_specs=[prev, _row_spec(tr, d), nxt, pl.BlockSpec((1, 6, d), lambda b, i: (b, 0, 0))]
                 + [_const_spec(a.shape) for a in consts],
        out_specs=[_row_spec(tr, wz), _row_spec(tr, wx), _row_spec(tr, wdt)],
        scratch_shapes=[pltpu.VMEM((tr + 2 * halo, d), BF16)]
                       + [pltpu.VMEM((2, tr + 2 * halo, SSD_IN_BLOCK), F32)] * 2,
        compiler_params=_cparams(2),
        name="ssd_in",
    )(x, x, x, mods, *consts)


def _split3(x):
    hi = x.astype(BF16)
    r1 = x - hi.astype(F32)
    mid = r1.astype(BF16)
    lo = (r1 - mid.astype(F32)).astype(BF16)
    return hi, mid, lo


def _ssd_scan_kernel(xs_ref, b_ref, c_ref, dt_ref, alog_ref, dskip_ref, h0_ref, y_ref, hout_ref,
                     yacc, st_f, st_b, g_s, gt_s, wt_s, cb_s, bt_s, tot_s, *, nc):
    q = SSD_CHUNK
    gw = xs_ref.shape[2]
    hpg = gw // SSD_HEADDIM
    pair_w = 2 * SSD_HEADDIM
    st_f[...] = h0_ref[0, 0, 0]
    st_b[...] = h0_ref[0, 0, 1]
    yacc[...] = xs_ref[0].astype(F32) * dskip_ref[0]

    ii = lax.broadcasted_iota(jnp.int32, (q, q), 0)
    jj = lax.broadcasted_iota(jnp.int32, (q, q), 1)
    keep = (ii >= jj, jj >= ii)
    tri_incl = jnp.where(ii >= jj, 1.0, 0.0).astype(BF16)
    lane = lax.broadcasted_iota(jnp.int32, (1, LANES), 1)
    fwd_lane = lane < hpg
    a_neg = jnp.where(lane < 2 * hpg, -jnp.exp(alog_ref[0]), 0.0)
    even_head = (lax.broadcasted_iota(jnp.int32, (q, gw), 1) % pair_w) < SSD_HEADDIM
    even_lane = lax.broadcasted_iota(jnp.int32, (F32_SUBLANES, pair_w), 1) < SSD_HEADDIM

    def prep(ci, carry):
        rows = pl.ds(pl.multiple_of(ci * q, q), q)
        bc, cc = b_ref[0, rows, :], c_ref[0, rows, :]
        dt = dt_ref[0, rows, :]
        a = dt * a_neg
        cs = functools.reduce(jnp.add, [_dot(tri_incl, part) for part in _split3(a)])
        tot = cs[q - 1:q, :]
        g = jnp.where(fwd_lane, cs, a - cs)
        w_state = dt * jnp.exp(jnp.where(fwd_lane, tot - cs, cs - a))
        state_off = jnp.where(fwd_lane, 0.0, -tot)
        g = g * LOG2_E
        g_s[ci] = g
        gt_s[ci] = jnp.concatenate([(g - jnp.log2(dt)).T,
                                    jnp.broadcast_to(state_off * LOG2_E, (q, LANES)).T], axis=1)
        wt_s[ci] = w_state.T
        cb_s[ci] = _dot_nt(cc, bc)
        bt_s[ci] = bc.astype(F32).T
        tot_s[ci] = jnp.broadcast_to(tot, (F32_SUBLANES, LANES))
        return carry

    lax.fori_loop(0, nc, prep, 0, unroll=2)

    def chunk(ci, d, st):
        rows = pl.ds(pl.multiple_of(ci * q, q), q)
        x = xs_ref[0, rows, :]
        x_sel = (jnp.where(even_head, x, jnp.zeros_like(x)), jnp.where(even_head, jnp.zeros_like(x), x))
        cc = c_ref[0, rows, :].astype(F32)
        g, cb, b_t, tot8 = g_s[ci], cb_s[ci], bt_s[ci], tot_s[ci]
        s_val = st[...]
        s_sel = (jnp.where(even_head, s_val, 0.0).astype(BF16), jnp.where(even_head, 0.0, s_val).astype(BF16))
        y_pairs, s_upd, decay = [], [], []
        for pr in range(hpg // 2):
            blk = slice(pr * pair_w, (pr + 1) * pair_w)
            acc, b_scaled, tots = None, [], []
            for e in range(2):
                k = d * hpg + 2 * pr + e
                gi = jnp.broadcast_to(g[:, k:k + 1], (q, q))
                ex_pos = jnp.exp2(jnp.where(keep[d], gi - gt_s[ci, k:k + 1, :q], -jnp.inf))
                ex_state = jnp.exp2(gi - gt_s[ci, k:k + 1, q:])
                lhs = jnp.concatenate([(cb * ex_pos).astype(BF16), (cc * ex_state).astype(BF16)], axis=1)
                rhs = jnp.concatenate([x_sel[e][:, blk], s_sel[e][:, blk]], axis=0)
                part = _dot(lhs, rhs)
                acc = part if acc is None else acc + part
                b_scaled.append((b_t * wt_s[ci, k:k + 1, :]).astype(BF16))
                tots.append(jnp.broadcast_to(tot8[:, k:k + 1], (F32_SUBLANES, pair_w)))
            y_pairs.append(acc)
            s_upd.append(_dot(jnp.concatenate(b_scaled, axis=1),
                              jnp.concatenate([x_sel[0][:, blk], x_sel[1][:, blk]], axis=0)))
            decay.append(jnp.where(even_lane, tots[0], tots[1]))
        yacc[rows, :] += jnp.concatenate(y_pairs, axis=1)
        chunk_decay = jnp.exp(jnp.concatenate(decay, axis=1))[0:1, :]
        st[...] = chunk_decay * s_val + jnp.concatenate(s_upd, axis=1)

    def body(ci, carry):
        chunk(ci, 0, st_f)
        chunk(nc - 1 - ci, 1, st_b)
        return carry

    lax.fori_loop(0, nc, body, 0)
    y_ref[0] = yacc[...].astype(BF16)
    hout_ref[0, 0, 0] = st_f[...]
    hout_ref[0, 0, 1] = st_b[...]


def _ssd_scan(xbc, dt, h0, p):
    nb, seq, _ = xbc.shape
    gw = h0.shape[4]
    n_x = SSD_GROUPS * gw // SSD_STATE
    st_spec = pl.BlockSpec((1, 1, 2, SSD_STATE, gw), lambda b, g: (b, g, 0, 0, 0))
    q, nc = SSD_CHUNK, seq // SSD_CHUNK
    return pl.pallas_call(
        functools.partial(_ssd_scan_kernel, nc=nc),
        out_shape=[jax.ShapeDtypeStruct((nb, seq, SSD_GROUPS * gw), BF16), jax.ShapeDtypeStruct(h0.shape, F32)],
        grid=(nb, SSD_GROUPS),
        in_specs=[pl.BlockSpec((1, seq, gw), lambda b, g: (b, 0, g)),
                  pl.BlockSpec((1, seq, SSD_STATE), lambda b, g: (b, 0, n_x + g)),
                  pl.BlockSpec((1, seq, SSD_STATE), lambda b, g: (b, 0, n_x + SSD_GROUPS + g)),
                  pl.BlockSpec((1, seq, LANES), lambda b, g: (b, 0, g)),
                  pl.BlockSpec((1, 1, LANES), lambda b, g: (g, 0, 0)),
                  pl.BlockSpec((1, 1, gw), lambda b, g: (g, 0, 0)),
                  st_spec],
        out_specs=[pl.BlockSpec((1, seq, gw), lambda b, g: (b, 0, g)), st_spec],
        scratch_shapes=[pltpu.VMEM((seq, gw), F32), pltpu.VMEM((SSD_STATE, gw), F32),
                        pltpu.VMEM((SSD_STATE, gw), F32)]
                       + [pltpu.VMEM((nc, rows, width), F32) for rows, width in
                          ((q, LANES), (LANES, 2 * q), (LANES, q), (q, q), (SSD_STATE, q),
                           (F32_SUBLANES, LANES))],
        compiler_params=_cparams(2),
        name="ssd_scan",
    )(xbc, xbc, xbc, dt, p["a_log"], p["d_skip"], h0)


def _ssd_out_kernel(y_ref, z_ref, x_ref, mod_ref, nw_ref, sn_ref, wo_ref, o_ref):
    y = y_ref[0].astype(F32) * _silu(z_ref[0].astype(F32))
    m = _dot(_rms(y, sn_ref[...]).astype(BF16), wo_ref[...])
    o_ref[0] = x_ref[0] + mod_ref[0, 2:3, :] * _rms(m, nw_ref[1:2, :])


def _ssd_out(y, z, x, mods, nw, p, *, tr):
    nb, seq, d = x.shape
    consts = [nw, p["norm"], p["w_out"]]
    return pl.pallas_call(
        _ssd_out_kernel,
        out_shape=jax.ShapeDtypeStruct((nb, seq, d), F32),
        grid=(nb, seq // tr),
        in_specs=[_row_spec(tr, y.shape[2]), _row_spec(tr, z.shape[2]), _row_spec(tr, d),
                  pl.BlockSpec((1, 6, d), lambda b, i: (b, 0, 0))] + [_const_spec(a.shape) for a in consts],
        out_specs=_row_spec(tr, d),
        compiler_params=_cparams(2),
        name="ssd_out",
    )(y, z, x, mods, *consts)


def _prep_hyb(w_in, q_norm, kv_norm, w_uq, w_ukv, sconv_w, w_out):
    d = w_in.shape[0]
    o_kr = MLA_Q_RANK + MLA_KV_RANK
    kr_block = jnp.pad(w_in[:, o_kr:o_kr + MLA_ROPE], ((0, 0), (MLA_NOPE, HEAD_PAD - MLA_NOPE - MLA_ROPE)))
    w_in_p = jnp.concatenate([w_in[:, :o_kr], kr_block, w_in[:, o_kr + MLA_ROPE:]], axis=1)
    pad_heads = lambda w, width: jnp.pad(w, ((0, 0), (0, 0), (0, HEAD_PAD - width))).reshape(w.shape[0], -1)
    qk = MLA_NOPE + MLA_ROPE
    w_uq_p = pad_heads(w_uq.reshape(MLA_Q_RANK, MLA_HEADS, qk), qk)
    w_ukv_h = w_ukv.reshape(MLA_KV_RANK, MLA_HEADS, MLA_NOPE + MLA_V)
    w_k = pad_heads(w_ukv_h[..., :MLA_NOPE], MLA_NOPE)
    w_v = pad_heads(w_ukv_h[..., MLA_NOPE:], MLA_V)
    n_attn = MLA_HEADS * MLA_V
    w_oa = jnp.pad(w_out[:n_attn].reshape(MLA_HEADS, MLA_V, d), ((0, 0), (0, HEAD_PAD - MLA_V), (0, 0)))
    return dict(w_in=w_in_p.astype(BF16), q_norm=q_norm[None, :], kv_norm=kv_norm[None, :],
                w_uq=w_uq_p.astype(BF16), w_k=w_k.astype(BF16), w_vt=w_v.T.astype(BF16),
                sconv_w=sconv_w, w_oa=w_oa.reshape(MLA_HEADS * HEAD_PAD, d).astype(BF16),
                w_os=w_out[n_attn:].astype(BF16))


def _prep_ssd(w_in, conv_w, conv_b, a_log, dt_bias, d_skip, norm_w, w_out):
    d = w_in.shape[0]
    inner = norm_w.shape[0]
    heads = a_log.shape[1]
    hpg = heads // SSD_GROUPS
    n_xbc = conv_w.shape[1]
    per_group = lambda t: jnp.pad(t.reshape(-1, 2, SSD_GROUPS, hpg).transpose(0, 2, 1, 3).reshape(-1, SSD_GROUPS, 2 * hpg),
                                  ((0, 0), (0, 0), (0, LANES - 2 * hpg)))
    w_dt = per_group(w_in[:, inner + n_xbc:]).reshape(d, SSD_GROUPS * LANES)
    return dict(w_z=w_in[:, :inner].astype(BF16), w_xbc=w_in[:, inner:inner + n_xbc].astype(BF16),
                w_dt=w_dt.astype(BF16), conv_w=conv_w, conv_b=conv_b[None, :],
                dt_bias=per_group(dt_bias.reshape(1, -1)).reshape(1, SSD_GROUPS * LANES),
                a_log=per_group(a_log.reshape(1, -1))[0][:, None, :],
                d_skip=jnp.repeat(d_skip.reshape(SSD_GROUPS, 1, hpg), SSD_HEADDIM, axis=2),
                norm=norm_w[None, :], w_out=w_out.astype(BF16))


def _rope_tables(rows):
    row = jnp.repeat(jnp.arange(rows, dtype=F32), GRID_W)
    col = jnp.tile(jnp.arange(GRID_W, dtype=F32), rows)
    nf = MLA_ROPE // 4
    inv = ROPE_THETA ** (-jnp.arange(nf, dtype=F32) / nf)
    ang = jnp.concatenate([row[:, None] * inv, col[:, None] * inv], axis=-1)
    cos, sin = jnp.cos(ang), jnp.sin(ang)
    n = ang.shape[0]
    half = MLA_ROPE // 2
    ones, zeros = jnp.ones((n, MLA_NOPE), F32), jnp.zeros((n, MLA_NOPE), F32)
    tail1, tail0 = jnp.ones((n, HEAD_PAD - MLA_NOPE - MLA_ROPE), F32), jnp.zeros((n, HEAD_PAD - MLA_NOPE - MLA_ROPE), F32)
    zh = jnp.zeros((n, half), F32)
    return (jnp.concatenate([ones, cos, cos, tail1], axis=1),
            jnp.concatenate([zeros, -sin, zh, tail0], axis=1),
            jnp.concatenate([zeros, zh, sin, tail0], axis=1))


def _identity_tables(n):
    return jnp.ones((n, HEAD_PAD), F32), jnp.zeros((n, HEAD_PAD), F32), jnp.zeros((n, HEAD_PAD), F32)


def _row_tile(seq):
    return min(seq, 512)


def kernel(x, c, ctx, c_ctx, mod_w, mod_b, norm_w, ffn_w_up, ffn_conv_w, ffn_w_down, hyb_w_in, mla_q_norm, mla_kv_norm, mla_w_uq, mla_w_ukv, sconv_w, hyb_w_out, ssd_w_in, ssd_conv_w, ssd_conv_b, ssd_a_log, ssd_dt_bias, ssd_d, ssd_norm, ssd_w_out):
    nb, seq, d = x.shape
    n_ctx = ctx.shape[1]
    depth = mod_w.shape[0]
    tr_l, tr_c = _row_tile(seq), _row_tile(n_ctx)
    tq_l, tq_c = min(seq, 2048), min(n_ctx, 256)

    n_rows = -(-(nb + 1) // F32_SUBLANES) * F32_SUBLANES
    cond = jnp.concatenate([c, c_ctx[None, :], jnp.zeros((n_rows - nb - 1, d), F32)], axis=0)
    mods = _modulation(cond, mod_w, mod_b).reshape(depth, n_rows, 6, d)

    tabs_l = _rope_tables(seq // GRID_W)
    tabs_c = _identity_tables(n_ctx)
    heads = ssd_a_log.shape[2]
    h0 = jnp.zeros((nb, SSD_GROUPS, 2, SSD_STATE, heads // SSD_GROUPS * SSD_HEADDIM), F32)

    for l in range(depth):
        last = l == depth - 1
        i = l // 2
        mod_l = mods[l, :nb]
        mod_c = jnp.broadcast_to(mods[l, nb:nb + 1], (nb, 6, d))
        nw = norm_w[l]
        if l % 2 == 0:
            p = _prep_hyb(hyb_w_in[i], mla_q_norm[i], mla_kv_norm[i], mla_w_uq[i], mla_w_ukv[i], sconv_w[i],
                          hyb_w_out[i])
            q_l, k_l, v_l, gb_l, u_l = _hyb_in(x, mod_l, nw, p, tabs_l, tr=tr_l)
            q_c, k_c, v_c, gb_c, u_c = _hyb_in(ctx, mod_c, nw, p, tabs_c, tr=tr_c)
            attn_l = _attention(q_l, [(k_c, v_c), (k_l, v_l)], tq=tq_l)
            x = _hyb_out(attn_l, gb_l, u_l, x, mod_l, nw, p, tr=tr_l)
            if not last:
                attn_c = _attention(q_c, [(k_c, v_c)], tq=tq_c)
                ctx = _hyb_out(attn_c, gb_c, u_c, ctx, mod_c, nw, p, tr=tr_c)
        else:
            p = _prep_ssd(ssd_w_in[i], ssd_conv_w[i], ssd_conv_b[i], ssd_a_log[i], ssd_dt_bias[i], ssd_d[i],
                          ssd_norm[i], ssd_w_out[i])
            z_c, xbc_c, dt_c = _ssd_in(ctx, mod_c, nw, p, tr=tr_c)
            y_c, h_fin = _ssd_scan(xbc_c, dt_c, h0, p)
            z_l, xbc_l, dt_l = _ssd_in(x, mod_l, nw, p, tr=tr_l)
            y_l, _ = _ssd_scan(xbc_l, dt_l, h_fin, p)
            x = _ssd_out(y_l, z_l, x, mod_l, nw, p, tr=tr_l)
            if not last:
                ctx = _ssd_out(y_c, z_c, ctx, mod_c, nw, p, tr=tr_c)
        wup, cw, wdn = ffn_w_up[l].astype(BF16), ffn_conv_w[l], ffn_w_down[l].astype(BF16)
        x = _ffn(x, mod_l, nw, wup, cw, wdn, tr=tr_l, ck=FFN_CHUNK)
        if not last:
            ctx = _ffn(ctx, mod_c, nw, wup, cw, wdn, tr=tr_c, ck=FFN_CHUNK)
    return x
```

```python
import functools

import jax
import jax.numpy as jnp
from jax import lax
from jax.experimental import pallas as pl
from jax.experimental.pallas import tpu as pltpu

F32 = jnp.float32
BF16 = jnp.bfloat16

EPS = 1e-6
LOG2_E = 1.4426950408889634
GRID_W = 64
ROPE_THETA = 10000.0
MLA_HEADS = 8
MLA_Q_RANK = 384
MLA_KV_RANK = 256
MLA_NOPE = 64
MLA_ROPE = 32
MLA_V = 64
MLA_SCALE = (MLA_NOPE + MLA_ROPE) ** -0.5
HEAD_PAD = 128
SC_WIDTH = 512
SSD_HEADDIM = 64
SSD_GROUPS = 4
SSD_STATE = 128
SSD_CHUNK = 128
FFN_CHUNK = 256
SSD_IN_BLOCK = 512
LANES = 128
F32_SUBLANES = 8
BF16_SUBLANES = 16
VMEM_LIMIT = 56 * 1024 * 1024


def _cparams(n_grid):
    return pltpu.CompilerParams(dimension_semantics=("arbitrary",) * n_grid, vmem_limit_bytes=VMEM_LIMIT)


def _rms(x, w):
    return x * lax.rsqrt(jnp.mean(x * x, axis=-1, keepdims=True) + EPS) * w


def _silu(x):
    return x * jax.nn.sigmoid(x)


def _dot(a, b):
    return jnp.dot(a, b, preferred_element_type=F32)


def _dot_nt(a, b):
    return lax.dot_general(a, b, (((1,), (1,)), ((), ())), preferred_element_type=F32)


def _const_spec(shape):
    nd = len(shape)
    return pl.BlockSpec(shape, lambda *_: (0,) * nd, pipeline_mode=pl.Buffered(1))


def _row_spec(tr, width):
    return pl.BlockSpec((1, tr, width), lambda b, i: (b, i, 0))


def _halo_specs(tr, halo, seq, width):
    per, last = tr // halo, seq // halo - 1
    prev = pl.BlockSpec((1, halo, width), lambda b, i: (b, jnp.maximum(i * per - 1, 0), 0))
    nxt = pl.BlockSpec((1, halo, width), lambda b, i: (b, jnp.minimum((i + 1) * per, last), 0))
    return prev, nxt


def _valid_rows(tr, halo, seq):
    rid = lax.broadcasted_iota(jnp.int32, (tr + 2 * halo, 1), 0) + (pl.program_id(1) * tr - halo)
    return (rid >= 0) & (rid < seq)


def _conv3(u, w):
    rows = u.shape[0]
    return pltpu.roll(u, 1, 0) * w[0:1] + u * w[1:2] + pltpu.roll(u, rows - 1, 0) * w[2:3]


def _mod_kernel(cond_ref, w_ref, b_ref, o_ref):
    cond = _silu(cond_ref[...])
    o_ref[0] = jnp.dot(cond, w_ref[0], preferred_element_type=F32, precision=lax.Precision.HIGHEST) + b_ref[0]


def _modulation(cond, mod_w, mod_b):
    depth, d, n = mod_w.shape
    rows = cond.shape[0]
    tn = 1536
    return pl.pallas_call(
        _mod_kernel,
        out_shape=jax.ShapeDtypeStruct((depth, rows, n), F32),
        grid=(depth, n // tn),
        in_specs=[pl.BlockSpec((rows, d), lambda l, j: (0, 0)),
                  pl.BlockSpec((1, d, tn), lambda l, j: (l, 0, j)),
                  pl.BlockSpec((1, 1, tn), lambda l, j: (l, 0, j))],
        out_specs=pl.BlockSpec((1, rows, tn), lambda l, j: (l, 0, j)),
        compiler_params=_cparams(2),
        name="modulation",
    )(cond, mod_w, mod_b.reshape(depth, 1, n))


def _ffn_kernel(xp_ref, x_ref, xn_ref, mod_ref, nw_ref, wup_ref, cw_ref, wdn_ref, o_ref, h_scr, u_scr,
                *, tr, halo, seq, nchunk, ck):
    x = x_ref[0]
    x_ext = jnp.concatenate([xp_ref[0], x, xn_ref[0]], axis=0)
    sh2, sc2, g2 = mod_ref[0, 3:4, :], mod_ref[0, 4:5, :], mod_ref[0, 5:6, :]
    h = _rms(x_ext, nw_ref[2:3, :]) * (1.0 + sc2) + sh2
    h_scr[...] = jnp.where(_valid_rows(tr, halo, seq), h, 0.0).astype(BF16)

    dff = wdn_ref.shape[1]

    def up(c, slot):
        for half in range(2):
            cols = slice(half * dff + c * ck, half * dff + (c + 1) * ck)
            u_scr[slot, half] = _dot(h_scr[...], wup_ref[0, :, cols])

    def gate(c, slot):
        ys = []
        for half in range(2):
            cols = slice(half * dff + c * ck, half * dff + (c + 1) * ck)
            ys.append(_conv3(u_scr[slot, half], cw_ref[0, :, cols])[halo:halo + tr])
        return (_silu(ys[0]) * ys[1]).astype(BF16)

    up(0, 0)
    gated = []
    for c in range(nchunk):
        if c + 1 < nchunk:
            up(c + 1, (c + 1) % 2)
        gated.append(gate(c, c % 2))
    f = _dot(jnp.concatenate(gated, axis=1), wdn_ref[0])
    o_ref[0] = x + g2 * _rms(f, nw_ref[3:4, :])


def _ffn(x, mods, nw, wup, cw, wdn, *, layer, tr, ck):
    nb, seq, d = x.shape
    halo = F32_SUBLANES
    nchunk = wdn.shape[1] // ck
    layer_spec = lambda a: pl.BlockSpec((1,) + a.shape[1:], lambda b, i: (layer, 0, 0), pipeline_mode=pl.Buffered(1))
    prev, nxt = _halo_specs(tr, halo, seq, d)
    kern = functools.partial(_ffn_kernel, tr=tr, halo=halo, seq=seq, nchunk=nchunk, ck=ck)
    return pl.pallas_call(
        kern,
        out_shape=jax.ShapeDtypeStruct((nb, seq, d), F32),
        grid=(nb, seq // tr),
        in_specs=[prev, _row_spec(tr, d), nxt,
                  pl.BlockSpec((1, 6, d), lambda b, i: (b, 0, 0)),
                  _const_spec(nw.shape), layer_spec(wup), layer_spec(cw), layer_spec(wdn)],
        out_specs=_row_spec(tr, d),
        scratch_shapes=[pltpu.VMEM((tr + 2 * halo, d), BF16), pltpu.VMEM((2, 2, tr + 2 * halo, ck), F32)],
        compiler_params=_cparams(2),
        name="ffn",
    )(x, x, x, mods, nw, wup, cw, wdn)


def _rope(x, c, sa, sb):
    w = x.shape[1]
    return x * c + pltpu.roll(x, w - MLA_ROPE // 2, 1) * sa + pltpu.roll(x, MLA_ROPE // 2, 1) * sb


def _hyb_in_kernel(x_ref, mod_ref, nw_ref, win_ref, qn_ref, kvn_ref, wuq_ref, wk_ref, wvt_ref,
                   rc_ref, rsa_ref, rsb_ref, q_ref, k_ref, vt_ref, gb_ref, u_ref):
    sh1, sc1 = mod_ref[0, 0:1, :], mod_ref[0, 1:2, :]
    h = (_rms(x_ref[0], nw_ref[0:1, :]) * (1.0 + sc1) + sh1).astype(BF16)
    o_kv, o_kr, o_gb = MLA_Q_RANK, MLA_Q_RANK + MLA_KV_RANK, MLA_Q_RANK + MLA_KV_RANK + HEAD_PAD
    o_gc, o_xv = o_gb + SC_WIDTH, o_gb + 2 * SC_WIDTH
    rc, rsa, rsb = rc_ref[...], rsa_ref[...], rsb_ref[...]
    tile = lambda t: jnp.concatenate([t] * MLA_HEADS, axis=1)

    cq = _dot(h, win_ref[:, 0:o_kv])
    qn = _rms(cq, qn_ref[...]).astype(BF16)
    q = _rope(_dot(qn, wuq_ref[...]), tile(rc), tile(rsa), tile(rsb))
    q_ref[0] = (q * (MLA_SCALE * LOG2_E)).astype(BF16)

    ckv = _dot(h, win_ref[:, o_kv:o_kr])
    kvn = _rms(ckv, kvn_ref[...]).astype(BF16)
    kr = _rope(_dot(h, win_ref[:, o_kr:o_gb]), rc, rsa, rsb)
    k_ref[0] = (_dot(kvn, wk_ref[...]) + tile(kr)).astype(BF16)
    vt = _dot_nt(wvt_ref[...], kvn)
    row = lax.broadcasted_iota(jnp.int32, vt.shape, 0)
    vt_ref[0] = jnp.where(row % HEAD_PAD == MLA_V, 1.0, vt).astype(BF16)

    gb_ref[0] = _dot(h, win_ref[:, o_gb:o_gc]).astype(BF16)
    gc = _dot(h, win_ref[:, o_gc:o_xv])
    xv = _dot(h, win_ref[:, o_xv:o_xv + SC_WIDTH])
    u_ref[0] = (gc * xv).astype(BF16)


def _hyb_in(x, mods, nw, p, rope_tabs, *, tr):
    nb, seq, d = x.shape
    hp = MLA_HEADS * HEAD_PAD
    consts = [nw, p["w_in"], p["q_norm"], p["kv_norm"], p["w_uq"], p["w_k"], p["w_vt"]]
    tab_spec = pl.BlockSpec((tr, HEAD_PAD), lambda b, i: (i, 0))
    row_out = lambda w: (jax.ShapeDtypeStruct((nb, seq, w), BF16), _row_spec(tr, w))
    outs = [row_out(hp), row_out(hp),
            (jax.ShapeDtypeStruct((nb, hp, seq), BF16), pl.BlockSpec((1, hp, tr), lambda b, i: (b, 0, i))),
            row_out(SC_WIDTH), row_out(SC_WIDTH)]
    return pl.pallas_call(
        _hyb_in_kernel,
        out_shape=[o[0] for o in outs],
        grid=(nb, seq // tr),
        in_specs=[_row_spec(tr, d), pl.BlockSpec((1, 6, d), lambda b, i: (b, 0, 0))]
                 + [_const_spec(a.shape) for a in consts] + [tab_spec] * 3,
        out_specs=[o[1] for o in outs],
        compiler_params=_cparams(2),
        name="hyb_in",
    )(x, mods, *consts, *rope_tabs)


def _attn_kernel(*refs, n_src, n_sub, n_heads):
    q_ref, o_ref = refs[0], refs[1 + 2 * n_src]
    sub = q_ref.shape[1] // n_sub
    tiles = [(hd, t) for hd in range(n_heads) for t in range(n_sub)]

    def qk(hd, t):
        q = q_ref[0, t * sub:(t + 1) * sub, hd * HEAD_PAD:(hd + 1) * HEAD_PAD]
        return [_dot_nt(refs[1 + 2 * s][0, :, hd * HEAD_PAD:(hd + 1) * HEAD_PAD], q) for s in range(n_src)]

    nxt = qk(*tiles[0])
    for i, (hd, t) in enumerate(tiles):
        scores, nxt = nxt, (qk(*tiles[i + 1]) if i + 1 < len(tiles) else None)
        m = functools.reduce(jnp.maximum, [jnp.max(s, axis=0, keepdims=True) for s in scores])
        o_t = None
        for s in range(n_src):
            v_t = refs[2 + 2 * s][0, hd * HEAD_PAD:(hd + 1) * HEAD_PAD, :]
            pv = _dot(v_t, jnp.exp2(scores[s] - m).astype(BF16))
            o_t = pv if o_t is None else o_t + pv
        o_t = o_t / o_t[MLA_V:MLA_V + 1, :]
        o_ref[0, t * sub:(t + 1) * sub, hd * HEAD_PAD:(hd + 1) * HEAD_PAD] = o_t.T.astype(BF16)


def _attention(q, kvs, *, tq, n_heads=2):
    nb, seq, hp = q.shape
    width = n_heads * HEAD_PAD
    in_specs = [pl.BlockSpec((1, tq, width), lambda b, h, i: (b, i, h))]
    args = [q]
    for k, vt in kvs:
        in_specs += [pl.BlockSpec((1, k.shape[1], width), lambda b, h, i: (b, 0, h)),
                     pl.BlockSpec((1, width, k.shape[1]), lambda b, h, i: (b, h, 0))]
        args += [k, vt]
    return pl.pallas_call(
        functools.partial(_attn_kernel, n_src=len(kvs), n_sub=max(tq // 512, 1), n_heads=n_heads),
        out_shape=jax.ShapeDtypeStruct((nb, seq, hp), BF16),
        grid=(nb, MLA_HEADS // n_heads, seq // tq),
        in_specs=in_specs,
        out_specs=pl.BlockSpec((1, tq, width), lambda b, h, i: (b, i, h)),
        compiler_params=_cparams(3),
        name="attention",
    )(*args)


def _hyb_out_kernel(attn_ref, gb_ref, up_ref, u_ref, un_ref, x_ref, mod_ref, nw_ref, cw_ref, woa_ref, wos_ref,
                    o_ref, *, tr, halo, seq):
    u_ext = jnp.concatenate([up_ref[0], u_ref[0], un_ref[0]], axis=0).astype(F32)
    u_ext = jnp.where(_valid_rows(tr, halo, seq), u_ext, 0.0)
    sc = gb_ref[0].astype(F32) * _conv3(u_ext, cw_ref[...])[halo:halo + tr]
    m = _dot(attn_ref[0], woa_ref[...]) + _dot(sc.astype(BF16), wos_ref[...])
    o_ref[0] = x_ref[0] + mod_ref[0, 2:3, :] * _rms(m, nw_ref[1:2, :])


def _hyb_out(attn, gb, u, x, mods, nw, p, *, tr):
    nb, seq, d = x.shape
    halo = BF16_SUBLANES
    prev, nxt = _halo_specs(tr, halo, seq, SC_WIDTH)
    consts = [nw, p["sconv_w"], p["w_oa"], p["w_os"]]
    return pl.pallas_call(
        functools.partial(_hyb_out_kernel, tr=tr, halo=halo, seq=seq),
        out_shape=jax.ShapeDtypeStruct((nb, seq, d), F32),
        grid=(nb, seq // tr),
        in_specs=[_row_spec(tr, attn.shape[2]), _row_spec(tr, SC_WIDTH), prev, _row_spec(tr, SC_WIDTH), nxt,
                  _row_spec(tr, d), pl.BlockSpec((1, 6, d), lambda b, i: (b, 0, 0))]
                 + [_const_spec(a.shape) for a in consts],
        out_specs=_row_spec(tr, d),
        compiler_params=_cparams(2),
        name="hyb_out",
    )(attn, gb, u, u, u, x, mods, *consts)


def _ssd_in_kernel(xp_ref, x_ref, xn_ref, mod_ref, nw_ref, wz_ref, wx_ref, wdt_ref, cw_ref, cb_ref, dtb_ref,
                   z_ref, xbc_ref, dt_ref, h_scr, ua_scr, ub_scr, *, tr, halo, seq, cn):
    x_ext = jnp.concatenate([xp_ref[0], x_ref[0], xn_ref[0]], axis=0)
    sh1, sc1 = mod_ref[0, 0:1, :], mod_ref[0, 1:2, :]
    h = _rms(x_ext, nw_ref[0:1, :]) * (1.0 + sc1) + sh1
    h_scr[...] = jnp.where(_valid_rows(tr, halo, seq), h, 0.0).astype(BF16)
    mid = slice(halo, halo + tr)

    def z_out(u, cols):
        z_ref[0, :, cols] = u[mid].astype(BF16)

    def xbc_out(u, cols):
        y = _conv3(u, cw_ref[:, cols])[mid] + cb_ref[:, cols]
        xbc_ref[0, :, cols] = _silu(y).astype(BF16)

    def dt_out(u, cols):
        dt = u[mid] + dtb_ref[...]
        dt_ref[0] = jnp.maximum(dt, 0.0) + jnp.log1p(jnp.exp(-jnp.abs(dt)))

    jobs = [(w_ref, slice(c * cn, (c + 1) * cn), out)
            for w_ref, out in ((wx_ref, xbc_out), (wz_ref, z_out), (wdt_ref, dt_out))
            for c in range(w_ref.shape[1] // cn)]
    dyn0 = jnp.minimum(pl.program_id(1), 0)
    bufs = (ua_scr, ub_scr)

    def project(i):
        w_ref, cols, _ = jobs[i]
        bufs[i % 2][dyn0] = _dot(h_scr[...], w_ref[:, cols])

    project(0)
    for i, (_, cols, out) in enumerate(jobs):
        if i + 1 < len(jobs):
            project(i + 1)
        out(bufs[i % 2][dyn0], cols)


def _ssd_in(x, mods, nw, p, *, tr):
    nb, seq, d = x.shape
    halo = F32_SUBLANES
    prev, nxt = _halo_specs(tr, halo, seq, d)
    consts = [nw, p["w_z"], p["w_xbc"], p["w_dt"], p["conv_w"], p["conv_b"], p["dt_bias"]]
    wz, wx, wdt = p["w_z"].shape[1], p["w_xbc"].shape[1], p["w_dt"].shape[1]
    return pl.pallas_call(
        functools.partial(_ssd_in_kernel, tr=tr, halo=halo, seq=seq, cn=SSD_IN_BLOCK),
        out_shape=[jax.ShapeDtypeStruct((nb, seq, wz), BF16), jax.ShapeDtypeStruct((nb, seq, wx), BF16),
                   jax.ShapeDtypeStruct((nb, seq, wdt), F32)],
        grid=(nb, seq // tr),
        in_specs=[prev, _row_spec(tr, d), nxt, pl.BlockSpec((1, 6, d), lambda b, i: (b, 0, 0))]
                 + [_const_spec(a.shape) for a in consts],
        out_specs=[_row_spec(tr, wz), _row_spec(tr, wx), _row_spec(tr, wdt)],
        scratch_shapes=[pltpu.VMEM((tr + 2 * halo, d), BF16)]
                       + [pltpu.VMEM((2, tr + 2 * halo, SSD_IN_BLOCK), F32)] * 2,
        compiler_params=_cparams(2),
        name="ssd_in",
    )(x, x, x, mods, *consts)


def _split3(x):
    hi = x.astype(BF16)
    r1 = x - hi.astype(F32)
    mid = r1.astype(BF16)
    lo = (r1 - mid.astype(F32)).astype(BF16)
    return hi, mid, lo


def _ssd_scan_kernel(xs_ref, b_ref, c_ref, dt_ref, alog_ref, dskip_ref, h0_ref, y_ref, hout_ref,
                     yacc, st_f, st_b, g_s, gt_s, wt_s, cb_s, bt_s, tot_s, *, nc):
    q = SSD_CHUNK
    gw = xs_ref.shape[2]
    hpg = gw // SSD_HEADDIM
    pair_w = 2 * SSD_HEADDIM
    st_f[...] = h0_ref[0, 0, 0]
    st_b[...] = h0_ref[0, 0, 1]
    yacc[...] = xs_ref[0].astype(F32) * dskip_ref[0]

    ii = lax.broadcasted_iota(jnp.int32, (q, q), 0)
    jj = lax.broadcasted_iota(jnp.int32, (q, q), 1)
    keep = (ii >= jj, jj >= ii)
    tri_incl = jnp.where(ii >= jj, 1.0, 0.0).astype(BF16)
    lane = lax.broadcasted_iota(jnp.int32, (1, LANES), 1)
    fwd_lane = lane < hpg
    a_neg = jnp.where(lane < 2 * hpg, -jnp.exp(alog_ref[0]), 0.0)
    even_head = (lax.broadcasted_iota(jnp.int32, (q, gw), 1) % pair_w) < SSD_HEADDIM
    even_lane = lax.broadcasted_iota(jnp.int32, (F32_SUBLANES, pair_w), 1) < SSD_HEADDIM

    def prep(ci, carry):
        rows = pl.ds(pl.multiple_of(ci * q, q), q)
        bc, cc = b_ref[0, rows, :], c_ref[0, rows, :]
        dt = dt_ref[0, rows, :]
        a = dt * a_neg
        cs = functools.reduce(jnp.add, [_dot(tri_incl, part) for part in _split3(a)])
        tot = cs[q - 1:q, :]
        g = jnp.where(fwd_lane, cs, a - cs)
        w_state = dt * jnp.exp(jnp.where(fwd_lane, tot - cs, cs - a))
        state_off = jnp.where(fwd_lane, 0.0, -tot)
        g = g * LOG2_E
        g_s[ci] = g
        gt_s[ci] = jnp.concatenate([(g - jnp.log2(dt)).T,
                                    jnp.broadcast_to(state_off * LOG2_E, (q, LANES)).T], axis=1)
        wt_s[ci] = w_state.T
        cb_s[ci] = _dot_nt(cc, bc)
        bt_s[ci] = bc.astype(F32).T
        tot_s[ci] = jnp.broadcast_to(tot, (F32_SUBLANES, LANES))
        return carry

    lax.fori_loop(0, nc, prep, 0, unroll=4)

    def chunk(ci, d, st):
        rows = pl.ds(pl.multiple_of(ci * q, q), q)
        x = xs_ref[0, rows, :]
        x_sel = (jnp.where(even_head, x, jnp.zeros_like(x)), jnp.where(even_head, jnp.zeros_like(x), x))
        cc = c_ref[0, rows, :].astype(F32)
        g, cb, b_t, tot8 = g_s[ci], cb_s[ci], bt_s[ci], tot_s[ci]
        s_val = st[...]
        s_sel = (jnp.where(even_head, s_val, 0.0).astype(BF16), jnp.where(even_head, 0.0, s_val).astype(BF16))
        y_pairs, s_upd, decay = [], [], []
        for pr in range(hpg // 2):
            blk = slice(pr * pair_w, (pr + 1) * pair_w)
            acc, b_scaled, tots = None, [], []
            for e in range(2):
                k = d * hpg + 2 * pr + e
                gi = jnp.broadcast_to(g[:, k:k + 1], (q, q))
                ex_pos = jnp.exp2(jnp.where(keep[d], gi - gt_s[ci, k:k + 1, :q], -jnp.inf))
                ex_state = jnp.exp2(gi - gt_s[ci, k:k + 1, q:])
                lhs = jnp.concatenate([(cb * ex_pos).astype(BF16), (cc * ex_state).astype(BF16)], axis=1)
                rhs = jnp.concatenate([x_sel[e][:, blk], s_sel[e][:, blk]], axis=0)
                part = _dot(lhs, rhs)
                acc = part if acc is None else acc + part
                b_scaled.append((b_t * wt_s[ci, k:k + 1, :]).astype(BF16))
                tots.append(jnp.broadcast_to(tot8[:, k:k + 1], (F32_SUBLANES, pair_w)))
            y_pairs.append(acc)
            s_upd.append(_dot(jnp.concatenate(b_scaled, axis=1),
                              jnp.concatenate([x_sel[0][:, blk], x_sel[1][:, blk]], axis=0)))
            decay.append(jnp.where(even_lane, tots[0], tots[1]))
        yacc[rows, :] += jnp.concatenate(y_pairs, axis=1)
        chunk_decay = jnp.exp(jnp.concatenate(decay, axis=1))[0:1, :]
        st[...] = chunk_decay * s_val + jnp.concatenate(s_upd, axis=1)

    def body(ci, carry):
        chunk(ci, 0, st_f)
        chunk(nc - 1 - ci, 1, st_b)
        return carry

    lax.fori_loop(0, nc, body, 0, unroll=4)
    y_ref[0] = yacc[...].astype(BF16)
    hout_ref[0, 0, 0] = st_f[...]
    hout_ref[0, 0, 1] = st_b[...]


def _ssd_scan(xbc, dt, h0, p):
    nb, seq, _ = xbc.shape
    gw = h0.shape[4]
    n_x = SSD_GROUPS * gw // SSD_STATE
    st_spec = pl.BlockSpec((1, 1, 2, SSD_STATE, gw), lambda b, g: (b, g, 0, 0, 0))
    q, nc = SSD_CHUNK, seq // SSD_CHUNK
    return pl.pallas_call(
        functools.partial(_ssd_scan_kernel, nc=nc),
        out_shape=[jax.ShapeDtypeStruct((nb, seq, SSD_GROUPS * gw), BF16), jax.ShapeDtypeStruct(h0.shape, F32)],
        grid=(nb, SSD_GROUPS),
        in_specs=[pl.BlockSpec((1, seq, gw), lambda b, g: (b, 0, g)),
                  pl.BlockSpec((1, seq, SSD_STATE), lambda b, g: (b, 0, n_x + g)),
                  pl.BlockSpec((1, seq, SSD_STATE), lambda b, g: (b, 0, n_x + SSD_GROUPS + g)),
                  pl.BlockSpec((1, seq, LANES), lambda b, g: (b, 0, g)),
                  pl.BlockSpec((1, 1, LANES), lambda b, g: (g, 0, 0)),
                  pl.BlockSpec((1, 1, gw), lambda b, g: (g, 0, 0)),
                  st_spec],
        out_specs=[pl.BlockSpec((1, seq, gw), lambda b, g: (b, 0, g)), st_spec],
        scratch_shapes=[pltpu.VMEM((seq, gw), F32), pltpu.VMEM((SSD_STATE, gw), F32),
                        pltpu.VMEM((SSD_STATE, gw), F32)]
                       + [pltpu.VMEM((nc, rows, width), F32) for rows, width in
                          ((q, LANES), (LANES, 2 * q), (LANES, q), (q, q), (SSD_STATE, q),
                           (F32_SUBLANES, LANES))],
        compiler_params=_cparams(2),
        name="ssd_scan",
    )(xbc, xbc, xbc, dt, p["a_log"], p["d_skip"], h0)


def _ssd_out_kernel(y_ref, z_ref, x_ref, mod_ref, nw_ref, sn_ref, wo_ref, o_ref):
    y = y_ref[0].astype(F32) * _silu(z_ref[0].astype(F32))
    m = _dot(_rms(y, sn_ref[...]).astype(BF16), wo_ref[...])
    o_ref[0] = x_ref[0] + mod_ref[0, 2:3, :] * _rms(m, nw_ref[1:2, :])


def _ssd_out(y, z, x, mods, nw, p, *, tr):
    nb, seq, d = x.shape
    consts = [nw, p["norm"], p["w_out"]]
    return pl.pallas_call(
        _ssd_out_kernel,
        out_shape=jax.ShapeDtypeStruct((nb, seq, d), F32),
        grid=(nb, seq // tr),
        in_specs=[_row_spec(tr, y.shape[2]), _row_spec(tr, z.shape[2]), _row_spec(tr, d),
                  pl.BlockSpec((1, 6, d), lambda b, i: (b, 0, 0))] + [_const_spec(a.shape) for a in consts],
        out_specs=_row_spec(tr, d),
        compiler_params=_cparams(2),
        name="ssd_out",
    )(y, z, x, mods, *consts)


def _prep_hyb(w_in, q_norm, kv_norm, w_uq, w_ukv, sconv_w, w_out):
    d = w_in.shape[0]
    o_kr = MLA_Q_RANK + MLA_KV_RANK
    kr_block = jnp.pad(w_in[:, o_kr:o_kr + MLA_ROPE], ((0, 0), (MLA_NOPE, HEAD_PAD - MLA_NOPE - MLA_ROPE)))
    w_in_p = jnp.concatenate([w_in[:, :o_kr], kr_block, w_in[:, o_kr + MLA_ROPE:]], axis=1)
    pad_heads = lambda w, width: jnp.pad(w, ((0, 0), (0, 0), (0, HEAD_PAD - width))).reshape(w.shape[0], -1)
    qk = MLA_NOPE + MLA_ROPE
    w_uq_p = pad_heads(w_uq.reshape(MLA_Q_RANK, MLA_HEADS, qk), qk)
    w_ukv_h = w_ukv.reshape(MLA_KV_RANK, MLA_HEADS, MLA_NOPE + MLA_V)
    w_k = pad_heads(w_ukv_h[..., :MLA_NOPE], MLA_NOPE)
    w_v = pad_heads(w_ukv_h[..., MLA_NOPE:], MLA_V)
    n_attn = MLA_HEADS * MLA_V
    w_oa = jnp.pad(w_out[:n_attn].reshape(MLA_HEADS, MLA_V, d), ((0, 0), (0, HEAD_PAD - MLA_V), (0, 0)))
    return dict(w_in=w_in_p.astype(BF16), q_norm=q_norm[None, :], kv_norm=kv_norm[None, :],
                w_uq=w_uq_p.astype(BF16), w_k=w_k.astype(BF16), w_vt=w_v.T.astype(BF16),
                sconv_w=sconv_w, w_oa=w_oa.reshape(MLA_HEADS * HEAD_PAD, d).astype(BF16),
                w_os=w_out[n_attn:].astype(BF16))


def _prep_ssd(w_in, conv_w, conv_b, a_log, dt_bias, d_skip, norm_w, w_out):
    d = w_in.shape[0]
    inner = norm_w.shape[0]
    heads = a_log.shape[1]
    hpg = heads // SSD_GROUPS
    n_xbc = conv_w.shape[1]
    per_group = lambda t: jnp.pad(t.reshape(-1, 2, SSD_GROUPS, hpg).transpose(0, 2, 1, 3).reshape(-1, SSD_GROUPS, 2 * hpg),
                                  ((0, 0), (0, 0), (0, LANES - 2 * hpg)))
    w_dt = per_group(w_in[:, inner + n_xbc:]).reshape(d, SSD_GROUPS * LANES)
    return dict(w_z=w_in[:, :inner].astype(BF16), w_xbc=w_in[:, inner:inner + n_xbc].astype(BF16),
                w_dt=w_dt.astype(BF16), conv_w=conv_w, conv_b=conv_b[None, :],
                dt_bias=per_group(dt_bias.reshape(1, -1)).reshape(1, SSD_GROUPS * LANES),
                a_log=per_group(a_log.reshape(1, -1))[0][:, None, :],
                d_skip=jnp.repeat(d_skip.reshape(SSD_GROUPS, 1, hpg), SSD_HEADDIM, axis=2),
                norm=norm_w[None, :], w_out=w_out.astype(BF16))


def _rope_tables(rows):
    row = jnp.repeat(jnp.arange(rows, dtype=F32), GRID_W)
    col = jnp.tile(jnp.arange(GRID_W, dtype=F32), rows)
    nf = MLA_ROPE // 4
    inv = ROPE_THETA ** (-jnp.arange(nf, dtype=F32) / nf)
    ang = jnp.concatenate([row[:, None] * inv, col[:, None] * inv], axis=-1)
    cos, sin = jnp.cos(ang), jnp.sin(ang)
    n = ang.shape[0]
    half = MLA_ROPE // 2
    ones, zeros = jnp.ones((n, MLA_NOPE), F32), jnp.zeros((n, MLA_NOPE), F32)
    tail1, tail0 = jnp.ones((n, HEAD_PAD - MLA_NOPE - MLA_ROPE), F32), jnp.zeros((n, HEAD_PAD - MLA_NOPE - MLA_ROPE), F32)
    zh = jnp.zeros((n, half), F32)
    return (jnp.concatenate([ones, cos, cos, tail1], axis=1),
            jnp.concatenate([zeros, -sin, zh, tail0], axis=1),
            jnp.concatenate([zeros, zh, sin, tail0], axis=1))


def _identity_tables(n):
    return jnp.ones((n, HEAD_PAD), F32), jnp.zeros((n, HEAD_PAD), F32), jnp.zeros((n, HEAD_PAD), F32)


def _row_tile(seq):
    return min(seq, 512)


def kernel(x, c, ctx, c_ctx, mod_w, mod_b, norm_w, ffn_w_up, ffn_conv_w, ffn_w_down, hyb_w_in, mla_q_norm, mla_kv_norm, mla_w_uq, mla_w_ukv, sconv_w, hyb_w_out, ssd_w_in, ssd_conv_w, ssd_conv_b, ssd_a_log, ssd_dt_bias, ssd_d, ssd_norm, ssd_w_out):
    nb, seq, d = x.shape
    n_ctx = ctx.shape[1]
    depth = mod_w.shape[0]
    tr_l, tr_c = _row_tile(seq), _row_tile(n_ctx)
    tq_l, tq_c = min(seq, 2048), min(n_ctx, 256)

    n_rows = -(-(nb + 1) // F32_SUBLANES) * F32_SUBLANES
    cond = jnp.concatenate([c, c_ctx[None, :], jnp.zeros((n_rows - nb - 1, d), F32)], axis=0)
    mods = _modulation(cond, mod_w, mod_b).reshape(depth, n_rows, 6, d)

    tabs_l = _rope_tables(seq // GRID_W)
    tabs_c = _identity_tables(n_ctx)
    heads = ssd_a_log.shape[2]
    h0 = jnp.zeros((nb, SSD_GROUPS, 2, SSD_STATE, heads // SSD_GROUPS * SSD_HEADDIM), F32)

    ffn_up, ffn_down = ffn_w_up.astype(BF16), ffn_w_down.astype(BF16)
    for l in range(depth):
        last = l == depth - 1
        i = l // 2
        mod_l = mods[l, :nb]
        mod_c = jnp.broadcast_to(mods[l, nb:nb + 1], (nb, 6, d))
        nw = norm_w[l]
        if l % 2 == 0:
            p = _prep_hyb(hyb_w_in[i], mla_q_norm[i], mla_kv_norm[i], mla_w_uq[i], mla_w_ukv[i], sconv_w[i],
                          hyb_w_out[i])
            q_l, k_l, v_l, gb_l, u_l = _hyb_in(x, mod_l, nw, p, tabs_l, tr=tr_l)
            q_c, k_c, v_c, gb_c, u_c = _hyb_in(ctx, mod_c, nw, p, tabs_c, tr=tr_c)
            attn_l = _attention(q_l, [(k_c, v_c), (k_l, v_l)], tq=tq_l)
            x = _hyb_out(attn_l, gb_l, u_l, x, mod_l, nw, p, tr=tr_l)
            if not last:
                attn_c = _attention(q_c, [(k_c, v_c)], tq=tq_c)
                ctx = _hyb_out(attn_c, gb_c, u_c, ctx, mod_c, nw, p, tr=tr_c)
        else:
            p = _prep_ssd(ssd_w_in[i], ssd_conv_w[i], ssd_conv_b[i], ssd_a_log[i], ssd_dt_bias[i], ssd_d[i],
                          ssd_norm[i], ssd_w_out[i])
            z_c, xbc_c, dt_c = _ssd_in(ctx, mod_c, nw, p, tr=tr_c)
            y_c, h_fin = _ssd_scan(xbc_c, dt_c, h0, p)
            z_l, xbc_l, dt_l = _ssd_in(x, mod_l, nw, p, tr=tr_l)
            y_l, _ = _ssd_scan(xbc_l, dt_l, h_fin, p)
            x = _ssd_out(y_l, z_l, x, mod_l, nw, p, tr=tr_l)
            if not last:
                ctx = _ssd_out(y_c, z_c, ctx, mod_c, nw, p, tr=tr_c)
        x = _ffn(x, mod_l, nw, ffn_up, ffn_conv_w, ffn_down, layer=l, tr=tr_l, ck=FFN_CHUNK)
        if not last:
            ctx = _ffn(ctx, mod_c, nw, ffn_up, ffn_conv_w, ffn_down, layer=l, tr=tr_c, ck=FFN_CHUNK)
    return x
```

```python
import functools

import jax
import jax.numpy as jnp
from jax import lax
from jax.experimental import pallas as pl
from jax.experimental.pallas import tpu as pltpu

F32 = jnp.float32
BF16 = jnp.bfloat16

EPS = 1e-6
LOG2_E = 1.4426950408889634
GRID_W = 64
ROPE_THETA = 10000.0
MLA_HEADS = 8
MLA_Q_RANK = 384
MLA_KV_RANK = 256
MLA_NOPE = 64
MLA_ROPE = 32
MLA_V = 64
MLA_SCALE = (MLA_NOPE + MLA_ROPE) ** -0.5
HEAD_PAD = 128
SC_WIDTH = 512
SSD_HEADDIM = 64
SSD_GROUPS = 4
SSD_STATE = 128
SSD_CHUNK = 128
FFN_CHUNK = 256
SSD_IN_BLOCK = 512
SCAN_UNROLL = 8
ATTN_HEADS_PER_STEP = 4
LANES = 128
F32_SUBLANES = 8
BF16_SUBLANES = 16
VMEM_LIMIT = 56 * 1024 * 1024


def _cparams(n_grid):
    return pltpu.CompilerParams(dimension_semantics=("arbitrary",) * n_grid, vmem_limit_bytes=VMEM_LIMIT)


def _rms(x, w):
    return x * lax.rsqrt(jnp.mean(x * x, axis=-1, keepdims=True) + EPS) * w


def _silu(x):
    return x * jax.nn.sigmoid(x)


def _dot(a, b):
    return jnp.dot(a, b, preferred_element_type=F32)


def _dot_nt(a, b):
    return lax.dot_general(a, b, (((1,), (1,)), ((), ())), preferred_element_type=F32)


def _const_spec(shape):
    nd = len(shape)
    return pl.BlockSpec(shape, lambda *_: (0,) * nd, pipeline_mode=pl.Buffered(1))


def _row_spec(tr, width):
    return pl.BlockSpec((1, tr, width), lambda b, i: (b, i, 0))


def _halo_specs(tr, halo, seq, width):
    per, last = tr // halo, seq // halo - 1
    prev = pl.BlockSpec((1, halo, width), lambda b, i: (b, jnp.maximum(i * per - 1, 0), 0))
    nxt = pl.BlockSpec((1, halo, width), lambda b, i: (b, jnp.minimum((i + 1) * per, last), 0))
    return prev, nxt


def _valid_rows(tr, halo, seq):
    rid = lax.broadcasted_iota(jnp.int32, (tr + 2 * halo, 1), 0) + (pl.program_id(1) * tr - halo)
    return (rid >= 0) & (rid < seq)


def _conv3(u, w):
    rows = u.shape[0]
    return pltpu.roll(u, 1, 0) * w[0:1] + u * w[1:2] + pltpu.roll(u, rows - 1, 0) * w[2:3]


def _mod_kernel(cond_ref, w_ref, b_ref, o_ref):
    cond = _silu(cond_ref[...])
    o_ref[0] = jnp.dot(cond, w_ref[0], preferred_element_type=F32, precision=lax.Precision.HIGHEST) + b_ref[0]


def _modulation(cond, mod_w, mod_b):
    depth, d, n = mod_w.shape
    rows = cond.shape[0]
    tn = 1536
    return pl.pallas_call(
        _mod_kernel,
        out_shape=jax.ShapeDtypeStruct((depth, rows, n), F32),
        grid=(depth, n // tn),
        in_specs=[pl.BlockSpec((rows, d), lambda l, j: (0, 0)),
                  pl.BlockSpec((1, d, tn), lambda l, j: (l, 0, j)),
                  pl.BlockSpec((1, 1, tn), lambda l, j: (l, 0, j))],
        out_specs=pl.BlockSpec((1, rows, tn), lambda l, j: (l, 0, j)),
        compiler_params=_cparams(2),
        name="modulation",
    )(cond, mod_w, mod_b.reshape(depth, 1, n))


def _ffn_kernel(xp_ref, x_ref, xn_ref, mod_ref, nw_ref, wup_ref, cw_ref, wdn_ref, o_ref, h_scr, u_scr,
                *, tr, halo, seq, nchunk, ck):
    x = x_ref[0]
    x_ext = jnp.concatenate([xp_ref[0], x, xn_ref[0]], axis=0)
    sh2, sc2, g2 = mod_ref[0, 3:4, :], mod_ref[0, 4:5, :], mod_ref[0, 5:6, :]
    h = _rms(x_ext, nw_ref[2:3, :]) * (1.0 + sc2) + sh2
    h_scr[...] = jnp.where(_valid_rows(tr, halo, seq), h, 0.0).astype(BF16)

    dff = wdn_ref.shape[1]

    def up(c, slot):
        for half in range(2):
            cols = slice(half * dff + c * ck, half * dff + (c + 1) * ck)
            u_scr[slot, half] = _dot(h_scr[...], wup_ref[0, :, cols])

    def gate(c, slot):
        ys = []
        for half in range(2):
            cols = slice(half * dff + c * ck, half * dff + (c + 1) * ck)
            ys.append(_conv3(u_scr[slot, half], cw_ref[0, :, cols])[halo:halo + tr])
        return (_silu(ys[0]) * ys[1]).astype(BF16)

    up(0, 0)
    gated = []
    for c in range(nchunk):
        if c + 1 < nchunk:
            up(c + 1, (c + 1) % 2)
        gated.append(gate(c, c % 2))
    f = _dot(jnp.concatenate(gated, axis=1), wdn_ref[0])
    o_ref[0] = x + g2 * _rms(f, nw_ref[3:4, :])


def _ffn(x, mods, nw, wup, cw, wdn, *, layer, tr, ck):
    nb, seq, d = x.shape
    halo = F32_SUBLANES
    nchunk = wdn.shape[1] // ck
    layer_spec = lambda a: pl.BlockSpec((1,) + a.shape[1:], lambda b, i: (layer, 0, 0), pipeline_mode=pl.Buffered(1))
    prev, nxt = _halo_specs(tr, halo, seq, d)
    kern = functools.partial(_ffn_kernel, tr=tr, halo=halo, seq=seq, nchunk=nchunk, ck=ck)
    return pl.pallas_call(
        kern,
        out_shape=jax.ShapeDtypeStruct((nb, seq, d), F32),
        grid=(nb, seq // tr),
        in_specs=[prev, _row_spec(tr, d), nxt,
                  pl.BlockSpec((1, 6, d), lambda b, i: (b, 0, 0)),
                  _const_spec(nw.shape), layer_spec(wup), layer_spec(cw), layer_spec(wdn)],
        out_specs=_row_spec(tr, d),
        scratch_shapes=[pltpu.VMEM((tr + 2 * halo, d), BF16), pltpu.VMEM((2, 2, tr + 2 * halo, ck), F32)],
        compiler_params=_cparams(2),
        name="ffn",
    )(x, x, x, mods, nw, wup, cw, wdn)


def _rope(x, c, sa, sb):
    w = x.shape[1]
    return x * c + pltpu.roll(x, w - MLA_ROPE // 2, 1) * sa + pltpu.roll(x, MLA_ROPE // 2, 1) * sb


def _hyb_in_kernel(x_ref, mod_ref, nw_ref, win_ref, qn_ref, kvn_ref, wuq_ref, wk_ref, wvt_ref,
                   rc_ref, rsa_ref, rsb_ref, q_ref, k_ref, vt_ref, gb_ref, u_ref):
    sh1, sc1 = mod_ref[0, 0:1, :], mod_ref[0, 1:2, :]
    h = (_rms(x_ref[0], nw_ref[0:1, :]) * (1.0 + sc1) + sh1).astype(BF16)
    o_kv, o_kr, o_gb = MLA_Q_RANK, MLA_Q_RANK + MLA_KV_RANK, MLA_Q_RANK + MLA_KV_RANK + HEAD_PAD
    o_gc, o_xv = o_gb + SC_WIDTH, o_gb + 2 * SC_WIDTH
    rc, rsa, rsb = rc_ref[...], rsa_ref[...], rsb_ref[...]
    tile = lambda t: jnp.concatenate([t] * MLA_HEADS, axis=1)

    cq = _dot(h, win_ref[:, 0:o_kv])
    qn = _rms(cq, qn_ref[...]).astype(BF16)
    q = _rope(_dot(qn, wuq_ref[...]), tile(rc), tile(rsa), tile(rsb))
    q_ref[0] = (q * (MLA_SCALE * LOG2_E)).astype(BF16)

    ckv = _dot(h, win_ref[:, o_kv:o_kr])
    kvn = _rms(ckv, kvn_ref[...]).astype(BF16)
    kr = _rope(_dot(h, win_ref[:, o_kr:o_gb]), rc, rsa, rsb)
    k_ref[0] = (_dot(kvn, wk_ref[...]) + tile(kr)).astype(BF16)
    vt = _dot_nt(wvt_ref[...], kvn)
    row = lax.broadcasted_iota(jnp.int32, vt.shape, 0)
    vt_ref[0] = jnp.where(row % HEAD_PAD == MLA_V, 1.0, vt).astype(BF16)

    gb_ref[0] = _dot(h, win_ref[:, o_gb:o_gc]).astype(BF16)
    gc = _dot(h, win_ref[:, o_gc:o_xv])
    xv = _dot(h, win_ref[:, o_xv:o_xv + SC_WIDTH])
    u_ref[0] = (gc * xv).astype(BF16)


def _hyb_in(x, mods, nw, p, rope_tabs, *, tr):
    nb, seq, d = x.shape
    hp = MLA_HEADS * HEAD_PAD
    consts = [nw, p["w_in"], p["q_norm"], p["kv_norm"], p["w_uq"], p["w_k"], p["w_vt"]]
    tab_spec = pl.BlockSpec((tr, HEAD_PAD), lambda b, i: (i, 0))
    row_out = lambda w: (jax.ShapeDtypeStruct((nb, seq, w), BF16), _row_spec(tr, w))
    outs = [row_out(hp), row_out(hp),
            (jax.ShapeDtypeStruct((nb, hp, seq), BF16), pl.BlockSpec((1, hp, tr), lambda b, i: (b, 0, i))),
            row_out(SC_WIDTH), row_out(SC_WIDTH)]
    return pl.pallas_call(
        _hyb_in_kernel,
        out_shape=[o[0] for o in outs],
        grid=(nb, seq // tr),
        in_specs=[_row_spec(tr, d), pl.BlockSpec((1, 6, d), lambda b, i: (b, 0, 0))]
                 + [_const_spec(a.shape) for a in consts] + [tab_spec] * 3,
        out_specs=[o[1] for o in outs],
        compiler_params=_cparams(2),
        name="hyb_in",
    )(x, mods, *consts, *rope_tabs)


def _attn_kernel(*refs, n_src, n_sub, n_heads):
    q_ref, o_ref = refs[0], refs[1 + 2 * n_src]
    sub = q_ref.shape[1] // n_sub
    tiles = [(hd, t) for hd in range(n_heads) for t in range(n_sub)]

    def qk(hd, t):
        q = q_ref[0, t * sub:(t + 1) * sub, hd * HEAD_PAD:(hd + 1) * HEAD_PAD]
        return [_dot_nt(refs[1 + 2 * s][0, :, hd * HEAD_PAD:(hd + 1) * HEAD_PAD], q) for s in range(n_src)]

    nxt = qk(*tiles[0])
    for i, (hd, t) in enumerate(tiles):
        scores, nxt = nxt, (qk(*tiles[i + 1]) if i + 1 < len(tiles) else None)
        m = functools.reduce(jnp.maximum, [jnp.max(s, axis=0, keepdims=True) for s in scores])
        o_t = None
        for s in range(n_src):
            v_t = refs[2 + 2 * s][0, hd * HEAD_PAD:(hd + 1) * HEAD_PAD, :]
            pv = _dot(v_t, jnp.exp2(scores[s] - m).astype(BF16))
            o_t = pv if o_t is None else o_t + pv
        o_t = o_t / o_t[MLA_V:MLA_V + 1, :]
        o_ref[0, t * sub:(t + 1) * sub, hd * HEAD_PAD:(hd + 1) * HEAD_PAD] = o_t.T.astype(BF16)


def _attention(q, kvs, *, tq, n_heads=ATTN_HEADS_PER_STEP):
    nb, seq, hp = q.shape
    width = n_heads * HEAD_PAD
    in_specs = [pl.BlockSpec((1, tq, width), lambda b, h, i: (b, i, h))]
    args = [q]
    for k, vt in kvs:
        in_specs += [pl.BlockSpec((1, k.shape[1], width), lambda b, h, i: (b, 0, h)),
                     pl.BlockSpec((1, width, k.shape[1]), lambda b, h, i: (b, h, 0))]
        args += [k, vt]
    return pl.pallas_call(
        functools.partial(_attn_kernel, n_src=len(kvs), n_sub=max(tq // 512, 1), n_heads=n_heads),
        out_shape=jax.ShapeDtypeStruct((nb, seq, hp), BF16),
        grid=(nb, MLA_HEADS // n_heads, seq // tq),
        in_specs=in_specs,
        out_specs=pl.BlockSpec((1, tq, width), lambda b, h, i: (b, i, h)),
        compiler_params=_cparams(3),
        name="attention",
    )(*args)


def _hyb_out_kernel(attn_ref, gb_ref, up_ref, u_ref, un_ref, x_ref, mod_ref, nw_ref, cw_ref, woa_ref, wos_ref,
                    o_ref, *, tr, halo, seq):
    u_ext = jnp.concatenate([up_ref[0], u_ref[0], un_ref[0]], axis=0).astype(F32)
    u_ext = jnp.where(_valid_rows(tr, halo, seq), u_ext, 0.0)
    sc = gb_ref[0].astype(F32) * _conv3(u_ext, cw_ref[...])[halo:halo + tr]
    m = _dot(attn_ref[0], woa_ref[...]) + _dot(sc.astype(BF16), wos_ref[...])
    o_ref[0] = x_ref[0] + mod_ref[0, 2:3, :] * _rms(m, nw_ref[1:2, :])


def _hyb_out(attn, gb, u, x, mods, nw, p, *, tr):
    nb, seq, d = x.shape
    halo = BF16_SUBLANES
    prev, nxt = _halo_specs(tr, halo, seq, SC_WIDTH)
    consts = [nw, p["sconv_w"], p["w_oa"], p["w_os"]]
    return pl.pallas_call(
        functools.partial(_hyb_out_kernel, tr=tr, halo=halo, seq=seq),
        out_shape=jax.ShapeDtypeStruct((nb, seq, d), F32),
        grid=(nb, seq // tr),
        in_specs=[_row_spec(tr, attn.shape[2]), _row_spec(tr, SC_WIDTH), prev, _row_spec(tr, SC_WIDTH), nxt,
                  _row_spec(tr, d), pl.BlockSpec((1, 6, d), lambda b, i: (b, 0, 0))]
                 + [_const_spec(a.shape) for a in consts],
        out_specs=_row_spec(tr, d),
        compiler_params=_cparams(2),
        name="hyb_out",
    )(attn, gb, u, u, u, x, mods, *consts)


def _ssd_in_kernel(xp_ref, x_ref, xn_ref, mod_ref, nw_ref, wz_ref, wx_ref, wdt_ref, cw_ref, cb_ref, dtb_ref,
                   z_ref, xbc_ref, dt_ref, h_scr, ua_scr, ub_scr, *, tr, halo, seq, cn):
    x_ext = jnp.concatenate([xp_ref[0], x_ref[0], xn_ref[0]], axis=0)
    sh1, sc1 = mod_ref[0, 0:1, :], mod_ref[0, 1:2, :]
    h = _rms(x_ext, nw_ref[0:1, :]) * (1.0 + sc1) + sh1
    h_scr[...] = jnp.where(_valid_rows(tr, halo, seq), h, 0.0).astype(BF16)
    mid = slice(halo, halo + tr)

    def z_out(u, cols):
        z_ref[0, :, cols] = u[mid].astype(BF16)

    def xbc_out(u, cols):
        y = _conv3(u, cw_ref[:, cols])[mid] + cb_ref[:, cols]
        xbc_ref[0, :, cols] = _silu(y).astype(BF16)

    def dt_out(u, cols):
        dt = u[mid] + dtb_ref[...]
        dt_ref[0] = jnp.maximum(dt, 0.0) + jnp.log1p(jnp.exp(-jnp.abs(dt)))

    jobs = [(w_ref, slice(c * cn, (c + 1) * cn), out)
            for w_ref, out in ((wx_ref, xbc_out), (wz_ref, z_out), (wdt_ref, dt_out))
            for c in range(w_ref.shape[1] // cn)]
    dyn0 = jnp.minimum(pl.program_id(1), 0)
    bufs = (ua_scr, ub_scr)

    def project(i):
        w_ref, cols, _ = jobs[i]
        bufs[i % 2][dyn0] = _dot(h_scr[...], w_ref[:, cols])

    project(0)
    for i, (_, cols, out) in enumerate(jobs):
        if i + 1 < len(jobs):
            project(i + 1)
        out(bufs[i % 2][dyn0], cols)


def _ssd_in(x, mods, nw, p, *, tr):
    nb, seq, d = x.shape
    halo = F32_SUBLANES
    prev, nxt = _halo_specs(tr, halo, seq, d)
    consts = [nw, p["w_z"], p["w_xbc"], p["w_dt"], p["conv_w"], p["conv_b"], p["dt_bias"]]
    wz, wx, wdt = p["w_z"].shape[1], p["w_xbc"].shape[1], p["w_dt"].shape[1]
    return pl.pallas_call(
        functools.partial(_ssd_in_kernel, tr=tr, halo=halo, seq=seq, cn=SSD_IN_BLOCK),
        out_shape=[jax.ShapeDtypeStruct((nb, seq, wz), BF16), jax.ShapeDtypeStruct((nb, seq, wx), BF16),
                   jax.ShapeDtypeStruct((nb, seq, wdt), F32)],
        grid=(nb, seq // tr),
        in_specs=[prev, _row_spec(tr, d), nxt, pl.BlockSpec((1, 6, d), lambda b, i: (b, 0, 0))]
                 + [_const_spec(a.shape) for a in consts],
        out_specs=[_row_spec(tr, wz), _row_spec(tr, wx), _row_spec(tr, wdt)],
        scratch_shapes=[pltpu.VMEM((tr + 2 * halo, d), BF16)]
                       + [pltpu.VMEM((2, tr + 2 * halo, SSD_IN_BLOCK), F32)] * 2,
        compiler_params=_cparams(2),
        name="ssd_in",
    )(x, x, x, mods, *consts)


def _split3(x):
    hi = x.astype(BF16)
    r1 = x - hi.astype(F32)
    mid = r1.astype(BF16)
    lo = (r1 - mid.astype(F32)).astype(BF16)
    return hi, mid, lo


def _ssd_scan_kernel(xs_ref, b_ref, c_ref, dt_ref, alog_ref, dskip_ref, h0_ref, y_ref, hout_ref,
                     yacc, st_f, st_b, g_s, gt_s, wt_s, cb_s, bt_s, tot_s, *, nc):
    q = SSD_CHUNK
    gw = xs_ref.shape[2]
    hpg = gw // SSD_HEADDIM
    pair_w = 2 * SSD_HEADDIM
    st_f[...] = h0_ref[0, 0, 0]
    st_b[...] = h0_ref[0, 0, 1]
    yacc[...] = xs_ref[0].astype(F32) * dskip_ref[0]

    ii = lax.broadcasted_iota(jnp.int32, (q, q), 0)
    jj = lax.broadcasted_iota(jnp.int32, (q, q), 1)
    keep = (ii >= jj, jj >= ii)
    tri_incl = jnp.where(ii >= jj, 1.0, 0.0).astype(BF16)
    lane = lax.broadcasted_iota(jnp.int32, (1, LANES), 1)
    fwd_lane = lane < hpg
    a_neg = jnp.where(lane < 2 * hpg, -jnp.exp(alog_ref[0]), 0.0)
    even_head = (lax.broadcasted_iota(jnp.int32, (q, gw), 1) % pair_w) < SSD_HEADDIM
    even_lane = lax.broadcasted_iota(jnp.int32, (F32_SUBLANES, pair_w), 1) < SSD_HEADDIM

    def prep(ci, carry):
        rows = pl.ds(pl.multiple_of(ci * q, q), q)
        bc, cc = b_ref[0, rows, :], c_ref[0, rows, :]
        dt = dt_ref[0, rows, :]
        a = dt * a_neg
        cs = functools.reduce(jnp.add, [_dot(tri_incl, part) for part in _split3(a)])
        tot = cs[q - 1:q, :]
        g = jnp.where(fwd_lane, cs, a - cs)
        w_state = dt * jnp.exp(jnp.where(fwd_lane, tot - cs, cs - a))
        state_off = jnp.where(fwd_lane, 0.0, -tot)
        g = g * LOG2_E
        g_s[ci] = g
        gt_s[ci] = jnp.concatenate([(g - jnp.log2(dt)).T,
                                    jnp.broadcast_to(state_off * LOG2_E, (q, LANES)).T], axis=1)
        wt_s[ci] = w_state.T
        cb_s[ci] = _dot_nt(cc, bc)
        bt_s[ci] = bc.astype(F32).T
        tot_s[ci] = jnp.broadcast_to(tot, (F32_SUBLANES, LANES))
        return carry

    lax.fori_loop(0, nc, prep, 0, unroll=SCAN_UNROLL)

    def chunk(ci, d, st):
        rows = pl.ds(pl.multiple_of(ci * q, q), q)
        x = xs_ref[0, rows, :]
        x_sel = (jnp.where(even_head, x, jnp.zeros_like(x)), jnp.where(even_head, jnp.zeros_like(x), x))
        cc = c_ref[0, rows, :].astype(F32)
        g, cb, b_t, tot8 = g_s[ci], cb_s[ci], bt_s[ci], tot_s[ci]
        s_val = st[...]
        s_sel = (jnp.where(even_head, s_val, 0.0).astype(BF16), jnp.where(even_head, 0.0, s_val).astype(BF16))
        y_pairs, s_upd, decay = [], [], []
        for pr in range(hpg // 2):
            blk = slice(pr * pair_w, (pr + 1) * pair_w)
            acc, b_scaled, tots = None, [], []
            for e in range(2):
                k = d * hpg + 2 * pr + e
                gi = jnp.broadcast_to(g[:, k:k + 1], (q, q))
                ex_pos = jnp.exp2(jnp.where(keep[d], gi - gt_s[ci, k:k + 1, :q], -jnp.inf))
                ex_state = jnp.exp2(gi - gt_s[ci, k:k + 1, q:])
                lhs = jnp.concatenate([(cb * ex_pos).astype(BF16), (cc * ex_state).astype(BF16)], axis=1)
                rhs = jnp.concatenate([x_sel[e][:, blk], s_sel[e][:, blk]], axis=0)
                part = _dot(lhs, rhs)
                acc = part if acc is None else acc + part
                b_scaled.append((b_t * wt_s[ci, k:k + 1, :]).astype(BF16))
                tots.append(jnp.broadcast_to(tot8[:, k:k + 1], (F32_SUBLANES, pair_w)))
            y_pairs.append(acc)
            s_upd.append(_dot(jnp.concatenate(b_scaled, axis=1),
                              jnp.concatenate([x_sel[0][:, blk], x_sel[1][:, blk]], axis=0)))
            decay.append(jnp.where(even_lane, tots[0], tots[1]))
        yacc[rows, :] += jnp.concatenate(y_pairs, axis=1)
        chunk_decay = jnp.exp(jnp.concatenate(decay, axis=1))[0:1, :]
        st[...] = chunk_decay * s_val + jnp.concatenate(s_upd, axis=1)

    def body(ci, carry):
        chunk(ci, 0, st_f)
        chunk(nc - 1 - ci, 1, st_b)
        return carry

    lax.fori_loop(0, nc, body, 0, unroll=SCAN_UNROLL)
    y_ref[0] = yacc[...].astype(BF16)
    hout_ref[0, 0, 0] = st_f[...]
    hout_ref[0, 0, 1] = st_b[...]


def _ssd_scan(xbc, dt, h0, p):
    nb, seq, _ = xbc.shape
    gw = h0.shape[4]
    n_x = SSD_GROUPS * gw // SSD_STATE
    st_spec = pl.BlockSpec((1, 1, 2, SSD_STATE, gw), lambda b, g: (b, g, 0, 0, 0))
    q, nc = SSD_CHUNK, seq // SSD_CHUNK
    return pl.pallas_call(
        functools.partial(_ssd_scan_kernel, nc=nc),
        out_shape=[jax.ShapeDtypeStruct((nb, seq, SSD_GROUPS * gw), BF16), jax.ShapeDtypeStruct(h0.shape, F32)],
        grid=(nb, SSD_GROUPS),
        in_specs=[pl.BlockSpec((1, seq, gw), lambda b, g: (b, 0, g)),
                  pl.BlockSpec((1, seq, SSD_STATE), lambda b, g: (b, 0, n_x + g)),
                  pl.BlockSpec((1, seq, SSD_STATE), lambda b, g: (b, 0, n_x + SSD_GROUPS + g)),
                  pl.BlockSpec((1, seq, LANES), lambda b, g: (b, 0, g)),
                  pl.BlockSpec((1, 1, LANES), lambda b, g: (g, 0, 0)),
                  pl.BlockSpec((1, 1, gw), lambda b, g: (g, 0, 0)),
                  st_spec],
        out_specs=[pl.BlockSpec((1, seq, gw), lambda b, g: (b, 0, g)), st_spec],
        scratch_shapes=[pltpu.VMEM((seq, gw), F32), pltpu.VMEM((SSD_STATE, gw), F32),
                        pltpu.VMEM((SSD_STATE, gw), F32)]
                       + [pltpu.VMEM((nc, rows, width), F32) for rows, width in
                          ((q, LANES), (LANES, 2 * q), (LANES, q), (q, q), (SSD_STATE, q),
                           (F32_SUBLANES, LANES))],
        compiler_params=_cparams(2),
        name="ssd_scan",
    )(xbc, xbc, xbc, dt, p["a_log"], p["d_skip"], h0)


def _ssd_out_kernel(y_ref, z_ref, x_ref, mod_ref, nw_ref, sn_ref, wo_ref, o_ref):
    v = y_ref[0].astype(F32) * _silu(z_ref[0].astype(F32))
    inv = lax.rsqrt(jnp.mean(v * v, axis=-1, keepdims=True) + EPS)
    m = _dot((v * sn_ref[...]).astype(BF16), wo_ref[...]) * inv
    o_ref[0] = x_ref[0] + mod_ref[0, 2:3, :] * _rms(m, nw_ref[1:2, :])


def _ssd_out(y, z, x, mods, nw, p, *, tr):
    nb, seq, d = x.shape
    consts = [nw, p["norm"], p["w_out"]]
    return pl.pallas_call(
        _ssd_out_kernel,
        out_shape=jax.ShapeDtypeStruct((nb, seq, d), F32),
        grid=(nb, seq // tr),
        in_specs=[_row_spec(tr, y.shape[2]), _row_spec(tr, z.shape[2]), _row_spec(tr, d),
                  pl.BlockSpec((1, 6, d), lambda b, i: (b, 0, 0))] + [_const_spec(a.shape) for a in consts],
        out_specs=_row_spec(tr, d),
        compiler_params=_cparams(2),
        name="ssd_out",
    )(y, z, x, mods, *consts)


def _prep_hyb(w_in, q_norm, kv_norm, w_uq, w_ukv, sconv_w, w_out):
    d = w_in.shape[0]
    o_kr = MLA_Q_RANK + MLA_KV_RANK
    kr_block = jnp.pad(w_in[:, o_kr:o_kr + MLA_ROPE], ((0, 0), (MLA_NOPE, HEAD_PAD - MLA_NOPE - MLA_ROPE)))
    w_in_p = jnp.concatenate([w_in[:, :o_kr], kr_block, w_in[:, o_kr + MLA_ROPE:]], axis=1)
    pad_heads = lambda w, width: jnp.pad(w, ((0, 0), (0, 0), (0, HEAD_PAD - width))).reshape(w.shape[0], -1)
    qk = MLA_NOPE + MLA_ROPE
    w_uq_p = pad_heads(w_uq.reshape(MLA_Q_RANK, MLA_HEADS, qk), qk)
    w_ukv_h = w_ukv.reshape(MLA_KV_RANK, MLA_HEADS, MLA_NOPE + MLA_V)
    w_k = pad_heads(w_ukv_h[..., :MLA_NOPE], MLA_NOPE)
    w_v = pad_heads(w_ukv_h[..., MLA_NOPE:], MLA_V)
    n_attn = MLA_HEADS * MLA_V
    w_oa = jnp.pad(w_out[:n_attn].reshape(MLA_HEADS, MLA_V, d), ((0, 0), (0, HEAD_PAD - MLA_V), (0, 0)))
    return dict(w_in=w_in_p.astype(BF16), q_norm=q_norm[None, :], kv_norm=kv_norm[None, :],
                w_uq=w_uq_p.astype(BF16), w_k=w_k.astype(BF16), w_vt=w_v.T.astype(BF16),
                sconv_w=sconv_w, w_oa=w_oa.reshape(MLA_HEADS * HEAD_PAD, d).astype(BF16),
                w_os=w_out[n_attn:].astype(BF16))


def _prep_ssd(w_in, conv_w, conv_b, a_log, dt_bias, d_skip, norm_w, w_out):
    d = w_in.shape[0]
    inner = norm_w.shape[0]
    heads = a_log.shape[1]
    hpg = heads // SSD_GROUPS
    n_xbc = conv_w.shape[1]
    per_group = lambda t: jnp.pad(t.reshape(-1, 2, SSD_GROUPS, hpg).transpose(0, 2, 1, 3).reshape(-1, SSD_GROUPS, 2 * hpg),
                                  ((0, 0), (0, 0), (0, LANES - 2 * hpg)))
    w_dt = per_group(w_in[:, inner + n_xbc:]).reshape(d, SSD_GROUPS * LANES)
    return dict(w_z=w_in[:, :inner].astype(BF16), w_xbc=w_in[:, inner:inner + n_xbc].astype(BF16),
                w_dt=w_dt.astype(BF16), conv_w=conv_w, conv_b=conv_b[None, :],
                dt_bias=per_group(dt_bias.reshape(1, -1)).reshape(1, SSD_GROUPS * LANES),
                a_log=per_group(a_log.reshape(1, -1))[0][:, None, :],
                d_skip=jnp.repeat(d_skip.reshape(SSD_GROUPS, 1, hpg), SSD_HEADDIM, axis=2),
                norm=norm_w[None, :], w_out=w_out.astype(BF16))


def _rope_tables(rows):
    row = jnp.repeat(jnp.arange(rows, dtype=F32), GRID_W)
    col = jnp.tile(jnp.arange(GRID_W, dtype=F32), rows)
    nf = MLA_ROPE // 4
    inv = ROPE_THETA ** (-jnp.arange(nf, dtype=F32) / nf)
    ang = jnp.concatenate([row[:, None] * inv, col[:, None] * inv], axis=-1)
    cos, sin = jnp.cos(ang), jnp.sin(ang)
    n = ang.shape[0]
    half = MLA_ROPE // 2
    ones, zeros = jnp.ones((n, MLA_NOPE), F32), jnp.zeros((n, MLA_NOPE), F32)
    tail1, tail0 = jnp.ones((n, HEAD_PAD - MLA_NOPE - MLA_ROPE), F32), jnp.zeros((n, HEAD_PAD - MLA_NOPE - MLA_ROPE), F32)
    zh = jnp.zeros((n, half), F32)
    return (jnp.concatenate([ones, cos, cos, tail1], axis=1),
            jnp.concatenate([zeros, -sin, zh, tail0], axis=1),
            jnp.concatenate([zeros, zh, sin, tail0], axis=1))


def _identity_tables(n):
    return jnp.ones((n, HEAD_PAD), F32), jnp.zeros((n, HEAD_PAD), F32), jnp.zeros((n, HEAD_PAD), F32)


def _row_tile(seq):
    return min(seq, 512)


def kernel(x, c, ctx, c_ctx, mod_w, mod_b, norm_w, ffn_w_up, ffn_conv_w, ffn_w_down, hyb_w_in, mla_q_norm, mla_kv_norm, mla_w_uq, mla_w_ukv, sconv_w, hyb_w_out, ssd_w_in, ssd_conv_w, ssd_conv_b, ssd_a_log, ssd_dt_bias, ssd_d, ssd_norm, ssd_w_out):
    nb, seq, d = x.shape
    n_ctx = ctx.shape[1]
    depth = mod_w.shape[0]
    tr_l, tr_c = _row_tile(seq), _row_tile(n_ctx)
    tq_l, tq_c = min(seq, 2048), min(n_ctx, 256)

    n_rows = -(-(nb + 1) // F32_SUBLANES) * F32_SUBLANES
    cond = jnp.concatenate([c, c_ctx[None, :], jnp.zeros((n_rows - nb - 1, d), F32)], axis=0)
    mods = _modulation(cond, mod_w, mod_b).reshape(depth, n_rows, 6, d)

    tabs_l = _rope_tables(seq // GRID_W)
    tabs_c = _identity_tables(n_ctx)
    heads = ssd_a_log.shape[2]
    h0 = jnp.zeros((nb, SSD_GROUPS, 2, SSD_STATE, heads // SSD_GROUPS * SSD_HEADDIM), F32)

    ffn_up, ffn_down = ffn_w_up.astype(BF16), ffn_w_down.astype(BF16)
    for l in range(depth):
        last = l == depth - 1
        i = l // 2
        mod_l = mods[l, :nb]
        mod_c = jnp.broadcast_to(mods[l, nb:nb + 1], (nb, 6, d))
        nw = norm_w[l]
        if l % 2 == 0:
            p = _prep_hyb(hyb_w_in[i], mla_q_norm[i], mla_kv_norm[i], mla_w_uq[i], mla_w_ukv[i], sconv_w[i],
                          hyb_w_out[i])
            q_l, k_l, v_l, gb_l, u_l = _hyb_in(x, mod_l, nw, p, tabs_l, tr=tr_l)
            q_c, k_c, v_c, gb_c, u_c = _hyb_in(ctx, mod_c, nw, p, tabs_c, tr=tr_c)
            attn_l = _attention(q_l, [(k_c, v_c), (k_l, v_l)], tq=tq_l)
            x = _hyb_out(attn_l, gb_l, u_l, x, mod_l, nw, p, tr=tr_l)
            if not last:
                attn_c = _attention(q_c, [(k_c, v_c)], tq=tq_c)
                ctx = _hyb_out(attn_c, gb_c, u_c, ctx, mod_c, nw, p, tr=tr_c)
        else:
            p = _prep_ssd(ssd_w_in[i], ssd_conv_w[i], ssd_conv_b[i], ssd_a_log[i], ssd_dt_bias[i], ssd_d[i],
                          ssd_norm[i], ssd_w_out[i])
            z_c, xbc_c, dt_c = _ssd_in(ctx, mod_c, nw, p, tr=tr_c)
            y_c, h_fin = _ssd_scan(xbc_c, dt_c, h0, p)
            z_l, xbc_l, dt_l = _ssd_in(x, mod_l, nw, p, tr=tr_l)
            y_l, _ = _ssd_scan(xbc_l, dt_l, h_fin, p)
            x = _ssd_out(y_l, z_l, x, mod_l, nw, p, tr=tr_l)
            if not last:
                ctx = _ssd_out(y_c, z_c, ctx, mod_c, nw, p, tr=tr_c)
        x = _ffn(x, mod_l, nw, ffn_up, ffn_conv_w, ffn_down, layer=l, tr=tr_l, ck=FFN_CHUNK)
        if not last:
            ctx = _ffn(ctx, mod_c, nw, ffn_up, ffn_conv_w, ffn_down, layer=l, tr=tr_c, ck=FFN_CHUNK)
    return x
```

```python
import functools

import jax
import jax.numpy as jnp
from jax import lax
from jax.experimental import pallas as pl
from jax.experimental.pallas import tpu as pltpu

F32 = jnp.float32
BF16 = jnp.bfloat16

EPS = 1e-6
LOG2_E = 1.4426950408889634
GRID_W = 64
ROPE_THETA = 10000.0
MLA_HEADS = 8
MLA_Q_RANK = 384
MLA_KV_RANK = 256
MLA_NOPE = 64
MLA_ROPE = 32
MLA_V = 64
MLA_SCALE = (MLA_NOPE + MLA_ROPE) ** -0.5
HEAD_PAD = 128
SC_WIDTH = 512
SSD_HEADDIM = 64
SSD_GROUPS = 4
SSD_STATE = 128
SSD_CHUNK = 128
FFN_CHUNK = 256
SSD_IN_BLOCK = 512
SCAN_UNROLL = 8
ATTN_HEADS_PER_STEP = 4
LANES = 128
F32_SUBLANES = 8
BF16_SUBLANES = 16
VMEM_LIMIT = 56 * 1024 * 1024


def _cparams(n_grid):
    return pltpu.CompilerParams(dimension_semantics=("arbitrary",) * n_grid, vmem_limit_bytes=VMEM_LIMIT)


def _rms(x, w):
    return x * lax.rsqrt(jnp.mean(x * x, axis=-1, keepdims=True) + EPS) * w


def _silu(x):
    return x * jax.nn.sigmoid(x)


def _dot(a, b):
    return jnp.dot(a, b, preferred_element_type=F32)


def _dot_nt(a, b):
    return lax.dot_general(a, b, (((1,), (1,)), ((), ())), preferred_element_type=F32)


def _const_spec(shape):
    nd = len(shape)
    return pl.BlockSpec(shape, lambda *_: (0,) * nd, pipeline_mode=pl.Buffered(1))


def _row_spec(tr, width):
    return pl.BlockSpec((1, tr, width), lambda b, i: (b, i, 0))


def _halo_specs(tr, halo, seq, width):
    per, last = tr // halo, seq // halo - 1
    prev = pl.BlockSpec((1, halo, width), lambda b, i: (b, jnp.maximum(i * per - 1, 0), 0))
    nxt = pl.BlockSpec((1, halo, width), lambda b, i: (b, jnp.minimum((i + 1) * per, last), 0))
    return prev, nxt


def _valid_rows(tr, halo, seq):
    rid = lax.broadcasted_iota(jnp.int32, (tr + 2 * halo, 1), 0) + (pl.program_id(1) * tr - halo)
    return (rid >= 0) & (rid < seq)


def _conv3(u, w):
    rows = u.shape[0]
    return pltpu.roll(u, 1, 0) * w[0:1] + u * w[1:2] + pltpu.roll(u, rows - 1, 0) * w[2:3]


def _mod_kernel(cond_ref, w_ref, b_ref, o_ref):
    cond = _silu(cond_ref[...])
    o_ref[0] = jnp.dot(cond, w_ref[0], preferred_element_type=F32, precision=lax.Precision.HIGHEST) + b_ref[0]


def _modulation(cond, mod_w, mod_b):
    depth, d, n = mod_w.shape
    rows = cond.shape[0]
    tn = 1536
    return pl.pallas_call(
        _mod_kernel,
        out_shape=jax.ShapeDtypeStruct((depth, rows, n), F32),
        grid=(depth, n // tn),
        in_specs=[pl.BlockSpec((rows, d), lambda l, j: (0, 0)),
                  pl.BlockSpec((1, d, tn), lambda l, j: (l, 0, j)),
                  pl.BlockSpec((1, 1, tn), lambda l, j: (l, 0, j))],
        out_specs=pl.BlockSpec((1, rows, tn), lambda l, j: (l, 0, j)),
        compiler_params=_cparams(2),
        name="modulation",
    )(cond, mod_w, mod_b.reshape(depth, 1, n))


def _ffn_kernel(xp_ref, x_ref, xn_ref, mod_ref, nw_ref, wup_ref, cw_ref, wdn_ref, o_ref, h_scr, u_scr,
                *, tr, halo, seq, nchunk, ck):
    x = x_ref[0]
    x_ext = jnp.concatenate([xp_ref[0], x, xn_ref[0]], axis=0)
    sh2, sc2, g2 = mod_ref[0, 3:4, :], mod_ref[0, 4:5, :], mod_ref[0, 5:6, :]
    h = _rms(x_ext, nw_ref[2:3, :]) * (1.0 + sc2) + sh2
    h_scr[...] = jnp.where(_valid_rows(tr, halo, seq), h, 0.0).astype(BF16)

    dff = wdn_ref.shape[1]

    def up(c, slot):
        for half in range(2):
            cols = slice(half * dff + c * ck, half * dff + (c + 1) * ck)
            u_scr[slot, half] = _dot(h_scr[...], wup_ref[0, :, cols])

    def gate(c, slot):
        strips = []
        for j in range(ck // LANES):
            ys = []
            for half in range(2):
                c0 = half * dff + c * ck + j * LANES
                u = u_scr[slot, half, :, j * LANES:(j + 1) * LANES]
                ys.append(_conv3(u, cw_ref[0, :, c0:c0 + LANES])[halo:halo + tr])
            strips.append((_silu(ys[0]) * ys[1]).astype(BF16))
        return jnp.concatenate(strips, axis=1)

    up(0, 0)
    gated = []
    for c in range(nchunk):
        if c + 1 < nchunk:
            up(c + 1, (c + 1) % 2)
        gated.append(gate(c, c % 2))
    f = _dot(jnp.concatenate(gated, axis=1), wdn_ref[0])
    o_ref[0] = x + g2 * _rms(f, nw_ref[3:4, :])


def _ffn(x, mods, nw, wup, cw, wdn, *, layer, tr, ck):
    nb, seq, d = x.shape
    halo = F32_SUBLANES
    nchunk = wdn.shape[1] // ck
    layer_spec = lambda a: pl.BlockSpec((1,) + a.shape[1:], lambda b, i: (layer, 0, 0), pipeline_mode=pl.Buffered(1))
    prev, nxt = _halo_specs(tr, halo, seq, d)
    kern = functools.partial(_ffn_kernel, tr=tr, halo=halo, seq=seq, nchunk=nchunk, ck=ck)
    return pl.pallas_call(
        kern,
        out_shape=jax.ShapeDtypeStruct((nb, seq, d), F32),
        grid=(nb, seq // tr),
        in_specs=[prev, _row_spec(tr, d), nxt,
                  pl.BlockSpec((1, 6, d), lambda b, i: (b, 0, 0)),
                  _const_spec(nw.shape), layer_spec(wup), layer_spec(cw), layer_spec(wdn)],
        out_specs=_row_spec(tr, d),
        scratch_shapes=[pltpu.VMEM((tr + 2 * halo, d), BF16), pltpu.VMEM((2, 2, tr + 2 * halo, ck), F32)],
        compiler_params=_cparams(2),
        name="ffn",
    )(x, x, x, mods, nw, wup, cw, wdn)


def _rope(x, c, sa, sb):
    w = x.shape[1]
    return x * c + pltpu.roll(x, w - MLA_ROPE // 2, 1) * sa + pltpu.roll(x, MLA_ROPE // 2, 1) * sb


def _hyb_in_kernel(x_ref, mod_ref, nw_ref, win_ref, qn_ref, kvn_ref, wuq_ref, wk_ref, wvt_ref,
                   rc_ref, rsa_ref, rsb_ref, q_ref, k_ref, vt_ref, gb_ref, u_ref):
    sh1, sc1 = mod_ref[0, 0:1, :], mod_ref[0, 1:2, :]
    h = (_rms(x_ref[0], nw_ref[0:1, :]) * (1.0 + sc1) + sh1).astype(BF16)
    o_kv, o_kr, o_gb = MLA_Q_RANK, MLA_Q_RANK + MLA_KV_RANK, MLA_Q_RANK + MLA_KV_RANK + HEAD_PAD
    o_gc, o_xv = o_gb + SC_WIDTH, o_gb + 2 * SC_WIDTH
    rc, rsa, rsb = rc_ref[...], rsa_ref[...], rsb_ref[...]
    tile = lambda t: jnp.concatenate([t] * MLA_HEADS, axis=1)

    cq = _dot(h, win_ref[:, 0:o_kv])
    qn = _rms(cq, qn_ref[...]).astype(BF16)
    q = _rope(_dot(qn, wuq_ref[...]), tile(rc), tile(rsa), tile(rsb))
    q_ref[0] = (q * (MLA_SCALE * LOG2_E)).astype(BF16)

    ckv = _dot(h, win_ref[:, o_kv:o_kr])
    kvn = _rms(ckv, kvn_ref[...]).astype(BF16)
    kr = _rope(_dot(h, win_ref[:, o_kr:o_gb]), rc, rsa, rsb)
    k_ref[0] = (_dot(kvn, wk_ref[...]) + tile(kr)).astype(BF16)
    vt = _dot_nt(wvt_ref[...], kvn)
    row = lax.broadcasted_iota(jnp.int32, vt.shape, 0)
    vt_ref[0] = jnp.where(row % HEAD_PAD == MLA_V, 1.0, vt).astype(BF16)

    gb_ref[0] = _dot(h, win_ref[:, o_gb:o_gc]).astype(BF16)
    gc = _dot(h, win_ref[:, o_gc:o_xv])
    xv = _dot(h, win_ref[:, o_xv:o_xv + SC_WIDTH])
    u_ref[0] = (gc * xv).astype(BF16)


def _hyb_in(x, mods, nw, p, rope_tabs, *, tr):
    nb, seq, d = x.shape
    hp = MLA_HEADS * HEAD_PAD
    consts = [nw, p["w_in"], p["q_norm"], p["kv_norm"], p["w_uq"], p["w_k"], p["w_vt"]]
    tab_spec = pl.BlockSpec((tr, HEAD_PAD), lambda b, i: (i, 0))
    row_out = lambda w: (jax.ShapeDtypeStruct((nb, seq, w), BF16), _row_spec(tr, w))
    outs = [row_out(hp), row_out(hp),
            (jax.ShapeDtypeStruct((nb, hp, seq), BF16), pl.BlockSpec((1, hp, tr), lambda b, i: (b, 0, i))),
            row_out(SC_WIDTH), row_out(SC_WIDTH)]
    return pl.pallas_call(
        _hyb_in_kernel,
        out_shape=[o[0] for o in outs],
        grid=(nb, seq // tr),
        in_specs=[_row_spec(tr, d), pl.BlockSpec((1, 6, d), lambda b, i: (b, 0, 0))]
                 + [_const_spec(a.shape) for a in consts] + [tab_spec] * 3,
        out_specs=[o[1] for o in outs],
        compiler_params=_cparams(2),
        name="hyb_in",
    )(x, mods, *consts, *rope_tabs)


def _attn_kernel(*refs, n_src, n_sub, n_heads):
    q_ref, o_ref = refs[0], refs[1 + 2 * n_src]
    sub = q_ref.shape[1] // n_sub
    tiles = [(hd, t) for hd in range(n_heads) for t in range(n_sub)]

    def qk(hd, t):
        q = q_ref[0, t * sub:(t + 1) * sub, hd * HEAD_PAD:(hd + 1) * HEAD_PAD]
        return [_dot_nt(refs[1 + 2 * s][0, :, hd * HEAD_PAD:(hd + 1) * HEAD_PAD], q) for s in range(n_src)]

    nxt = qk(*tiles[0])
    for i, (hd, t) in enumerate(tiles):
        scores, nxt = nxt, (qk(*tiles[i + 1]) if i + 1 < len(tiles) else None)
        m = functools.reduce(jnp.maximum, [jnp.max(s, axis=0, keepdims=True) for s in scores])
        o_t = None
        for s in range(n_src):
            v_t = refs[2 + 2 * s][0, hd * HEAD_PAD:(hd + 1) * HEAD_PAD, :]
            pv = _dot(v_t, jnp.exp2(scores[s] - m).astype(BF16))
            o_t = pv if o_t is None else o_t + pv
        o_t = o_t / o_t[MLA_V:MLA_V + 1, :]
        o_ref[0, t * sub:(t + 1) * sub, hd * HEAD_PAD:(hd + 1) * HEAD_PAD] = o_t.T.astype(BF16)


def _attention(q, kvs, *, tq, n_heads=ATTN_HEADS_PER_STEP):
    nb, seq, hp = q.shape
    width = n_heads * HEAD_PAD
    in_specs = [pl.BlockSpec((1, tq, width), lambda b, h, i: (b, i, h))]
    args = [q]
    for k, vt in kvs:
        in_specs += [pl.BlockSpec((1, k.shape[1], width), lambda b, h, i: (b, 0, h)),
                     pl.BlockSpec((1, width, k.shape[1]), lambda b, h, i: (b, h, 0))]
        args += [k, vt]
    return pl.pallas_call(
        functools.partial(_attn_kernel, n_src=len(kvs), n_sub=max(tq // 512, 1), n_heads=n_heads),
        out_shape=jax.ShapeDtypeStruct((nb, seq, hp), BF16),
        grid=(nb, MLA_HEADS // n_heads, seq // tq),
        in_specs=in_specs,
        out_specs=pl.BlockSpec((1, tq, width), lambda b, h, i: (b, i, h)),
        compiler_params=_cparams(3),
        name="attention",
    )(*args)


def _hyb_out_kernel(attn_ref, gb_ref, up_ref, u_ref, un_ref, x_ref, mod_ref, nw_ref, cw_ref, woa_ref, wos_ref,
                    o_ref, *, tr, halo, seq):
    u_ext = jnp.concatenate([up_ref[0], u_ref[0], un_ref[0]], axis=0).astype(F32)
    u_ext = jnp.where(_valid_rows(tr, halo, seq), u_ext, 0.0)
    sc = gb_ref[0].astype(F32) * _conv3(u_ext, cw_ref[...])[halo:halo + tr]
    m = _dot(attn_ref[0], woa_ref[...]) + _dot(sc.astype(BF16), wos_ref[...])
    o_ref[0] = x_ref[0] + mod_ref[0, 2:3, :] * _rms(m, nw_ref[1:2, :])


def _hyb_out(attn, gb, u, x, mods, nw, p, *, tr):
    nb, seq, d = x.shape
    halo = BF16_SUBLANES
    prev, nxt = _halo_specs(tr, halo, seq, SC_WIDTH)
    consts = [nw, p["sconv_w"], p["w_oa"], p["w_os"]]
    return pl.pallas_call(
        functools.partial(_hyb_out_kernel, tr=tr, halo=halo, seq=seq),
        out_shape=jax.ShapeDtypeStruct((nb, seq, d), F32),
        grid=(nb, seq // tr),
        in_specs=[_row_spec(tr, attn.shape[2]), _row_spec(tr, SC_WIDTH), prev, _row_spec(tr, SC_WIDTH), nxt,
                  _row_spec(tr, d), pl.BlockSpec((1, 6, d), lambda b, i: (b, 0, 0))]
                 + [_const_spec(a.shape) for a in consts],
        out_specs=_row_spec(tr, d),
        compiler_params=_cparams(2),
        name="hyb_out",
    )(attn, gb, u, u, u, x, mods, *consts)


def _ssd_in_kernel(xp_ref, x_ref, xn_ref, mod_ref, nw_ref, wz_ref, wx_ref, wdt_ref, cw_ref, cb_ref, dtb_ref,
                   z_ref, xbc_ref, dt_ref, h_scr, ua_scr, ub_scr, *, tr, halo, seq, cn):
    x_ext = jnp.concatenate([xp_ref[0], x_ref[0], xn_ref[0]], axis=0)
    sh1, sc1 = mod_ref[0, 0:1, :], mod_ref[0, 1:2, :]
    h = _rms(x_ext, nw_ref[0:1, :]) * (1.0 + sc1) + sh1
    h_scr[...] = jnp.where(_valid_rows(tr, halo, seq), h, 0.0).astype(BF16)
    mid = slice(halo, halo + tr)

    def z_out(u, cols):
        z_ref[0, :, cols] = u[mid].astype(BF16)

    def xbc_out(u, cols):
        y = _conv3(u, cw_ref[:, cols])[mid] + cb_ref[:, cols]
        xbc_ref[0, :, cols] = _silu(y).astype(BF16)

    def dt_out(u, cols):
        dt = u[mid] + dtb_ref[...]
        dt_ref[0] = jnp.maximum(dt, 0.0) + jnp.log1p(jnp.exp(-jnp.abs(dt)))

    jobs = [(w_ref, slice(c * cn, (c + 1) * cn), out)
            for w_ref, out in ((wx_ref, xbc_out), (wz_ref, z_out), (wdt_ref, dt_out))
            for c in range(w_ref.shape[1] // cn)]
    dyn0 = jnp.minimum(pl.program_id(1), 0)
    bufs = (ua_scr, ub_scr)

    def project(i):
        w_ref, cols, _ = jobs[i]
        bufs[i % 2][dyn0] = _dot(h_scr[...], w_ref[:, cols])

    project(0)
    for i, (_, cols, out) in enumerate(jobs):
        if i + 1 < len(jobs):
            project(i + 1)
        out(bufs[i % 2][dyn0], cols)


def _ssd_in(x, mods, nw, p, *, tr):
    nb, seq, d = x.shape
    halo = F32_SUBLANES
    prev, nxt = _halo_specs(tr, halo, seq, d)
    consts = [nw, p["w_z"], p["w_xbc"], p["w_dt"], p["conv_w"], p["conv_b"], p["dt_bias"]]
    wz, wx, wdt = p["w_z"].shape[1], p["w_xbc"].shape[1], p["w_dt"].shape[1]
    return pl.pallas_call(
        functools.partial(_ssd_in_kernel, tr=tr, halo=halo, seq=seq, cn=SSD_IN_BLOCK),
        out_shape=[jax.ShapeDtypeStruct((nb, seq, wz), BF16), jax.ShapeDtypeStruct((nb, seq, wx), BF16),
                   jax.ShapeDtypeStruct((nb, seq, wdt), F32)],
        grid=(nb, seq // tr),
        in_specs=[prev, _row_spec(tr, d), nxt, pl.BlockSpec((1, 6, d), lambda b, i: (b, 0, 0))]
                 + [_const_spec(a.shape) for a in consts],
        out_specs=[_row_spec(tr, wz), _row_spec(tr, wx), _row_spec(tr, wdt)],
        scratch_shapes=[pltpu.VMEM((tr + 2 * halo, d), BF16)]
                       + [pltpu.VMEM((2, tr + 2 * halo, SSD_IN_BLOCK), F32)] * 2,
        compiler_params=_cparams(2),
        name="ssd_in",
    )(x, x, x, mods, *consts)


def _split3(x):
    hi = x.astype(BF16)
    r1 = x - hi.astype(F32)
    mid = r1.astype(BF16)
    lo = (r1 - mid.astype(F32)).astype(BF16)
    return hi, mid, lo


def _ssd_scan_kernel(xs_ref, b_ref, c_ref, dt_ref, alog_ref, dskip_ref, h0_ref, y_ref, hout_ref,
                     yacc, st_f, st_b, g_s, gt_s, wt_s, cb_s, bt_s, tot_s, *, nc):
    q = SSD_CHUNK
    gw = xs_ref.shape[2]
    hpg = gw // SSD_HEADDIM
    pair_w = 2 * SSD_HEADDIM
    st_f[...] = h0_ref[0, 0, 0]
    st_b[...] = h0_ref[0, 0, 1]
    yacc[...] = xs_ref[0].astype(F32) * dskip_ref[0]

    ii = lax.broadcasted_iota(jnp.int32, (q, q), 0)
    jj = lax.broadcasted_iota(jnp.int32, (q, q), 1)
    keep = (ii >= jj, jj >= ii)
    tri_incl = jnp.where(ii >= jj, 1.0, 0.0).astype(BF16)
    lane = lax.broadcasted_iota(jnp.int32, (1, LANES), 1)
    fwd_lane = lane < hpg
    a_neg = jnp.where(lane < 2 * hpg, -jnp.exp(alog_ref[0]), 0.0)
    even_head = (lax.broadcasted_iota(jnp.int32, (q, gw), 1) % pair_w) < SSD_HEADDIM
    even_lane = lax.broadcasted_iota(jnp.int32, (F32_SUBLANES, pair_w), 1) < SSD_HEADDIM

    def prep(ci, carry):
        rows = pl.ds(pl.multiple_of(ci * q, q), q)
        bc, cc = b_ref[0, rows, :], c_ref[0, rows, :]
        dt = dt_ref[0, rows, :]
        a = dt * a_neg
        cs = functools.reduce(jnp.add, [_dot(tri_incl, part) for part in _split3(a)])
        tot = cs[q - 1:q, :]
        g = jnp.where(fwd_lane, cs, a - cs)
        w_state = dt * jnp.exp(jnp.where(fwd_lane, tot - cs, cs - a))
        state_off = jnp.where(fwd_lane, 0.0, -tot)
        g = g * LOG2_E
        g_s[ci] = g
        gt_s[ci] = jnp.concatenate([(g - jnp.log2(dt)).T,
                                    jnp.broadcast_to(state_off * LOG2_E, (q, LANES)).T], axis=1)
        wt_s[ci] = w_state.T
        cb_s[ci] = _dot_nt(cc, bc)
        bt_s[ci] = bc.astype(F32).T
        tot_s[ci] = jnp.broadcast_to(tot, (F32_SUBLANES, LANES))
        return carry

    lax.fori_loop(0, nc, prep, 0, unroll=SCAN_UNROLL)

    def chunk(ci, d, st):
        rows = pl.ds(pl.multiple_of(ci * q, q), q)
        x = xs_ref[0, rows, :]
        x_sel = (jnp.where(even_head, x, jnp.zeros_like(x)), jnp.where(even_head, jnp.zeros_like(x), x))
        cc = c_ref[0, rows, :].astype(F32)
        g, cb, b_t, tot8 = g_s[ci], cb_s[ci], bt_s[ci], tot_s[ci]
        s_val = st[...]
        s_sel = (jnp.where(even_head, s_val, 0.0).astype(BF16), jnp.where(even_head, 0.0, s_val).astype(BF16))
        y_pairs, s_upd, decay = [], [], []
        for pr in range(hpg // 2):
            blk = slice(pr * pair_w, (pr + 1) * pair_w)
            acc, b_scaled, tots = None, [], []
            for e in range(2):
                k = d * hpg + 2 * pr + e
                gi = jnp.broadcast_to(g[:, k:k + 1], (q, q))
                ex_pos = jnp.exp2(jnp.where(keep[d], gi - gt_s[ci, k:k + 1, :q], -jnp.inf))
                ex_state = jnp.exp2(gi - gt_s[ci, k:k + 1, q:])
                lhs = jnp.concatenate([(cb * ex_pos).astype(BF16), (cc * ex_state).astype(BF16)], axis=1)
                rhs = jnp.concatenate([x_sel[e][:, blk], s_sel[e][:, blk]], axis=0)
                part = _dot(lhs, rhs)
                acc = part if acc is None else acc + part
                b_scaled.append((b_t * wt_s[ci, k:k + 1, :]).astype(BF16))
                tots.append(jnp.broadcast_to(tot8[:, k:k + 1], (F32_SUBLANES, pair_w)))
            y_pairs.append(acc)
            s_upd.append(_dot(jnp.concatenate(b_scaled, axis=1),
                              jnp.concatenate([x_sel[0][:, blk], x_sel[1][:, blk]], axis=0)))
            decay.append(jnp.where(even_lane, tots[0], tots[1]))
        yacc[rows, :] += jnp.concatenate(y_pairs, axis=1)
        chunk_decay = jnp.exp(jnp.concatenate(decay, axis=1))[0:1, :]
        st[...] = chunk_decay * s_val + jnp.concatenate(s_upd, axis=1)

    def body(ci, carry):
        chunk(ci, 0, st_f)
        chunk(nc - 1 - ci, 1, st_b)
        return carry

    lax.fori_loop(0, nc, body, 0, unroll=SCAN_UNROLL)
    y_ref[0] = yacc[...].astype(BF16)
    hout_ref[0, 0, 0] = st_f[...]
    hout_ref[0, 0, 1] = st_b[...]


def _ssd_scan(xbc, dt, h0, p):
    nb, seq, _ = xbc.shape
    gw = h0.shape[4]
    n_x = SSD_GROUPS * gw // SSD_STATE
    st_spec = pl.BlockSpec((1, 1, 2, SSD_STATE, gw), lambda b, g: (b, g, 0, 0, 0))
    q, nc = SSD_CHUNK, seq // SSD_CHUNK
    return pl.pallas_call(
        functools.partial(_ssd_scan_kernel, nc=nc),
        out_shape=[jax.ShapeDtypeStruct((nb, seq, SSD_GROUPS * gw), BF16), jax.ShapeDtypeStruct(h0.shape, F32)],
        grid=(nb, SSD_GROUPS),
        in_specs=[pl.BlockSpec((1, seq, gw), lambda b, g: (b, 0, g)),
                  pl.BlockSpec((1, seq, SSD_STATE), lambda b, g: (b, 0, n_x + g)),
                  pl.BlockSpec((1, seq, SSD_STATE), lambda b, g: (b, 0, n_x + SSD_GROUPS + g)),
                  pl.BlockSpec((1, seq, LANES), lambda b, g: (b, 0, g)),
                  pl.BlockSpec((1, 1, LANES), lambda b, g: (g, 0, 0)),
                  pl.BlockSpec((1, 1, gw), lambda b, g: (g, 0, 0)),
                  st_spec],
        out_specs=[pl.BlockSpec((1, seq, gw), lambda b, g: (b, 0, g)), st_spec],
        scratch_shapes=[pltpu.VMEM((seq, gw), F32), pltpu.VMEM((SSD_STATE, gw), F32),
                        pltpu.VMEM((SSD_STATE, gw), F32)]
                       + [pltpu.VMEM((nc, rows, width), F32) for rows, width in
                          ((q, LANES), (LANES, 2 * q), (LANES, q), (q, q), (SSD_STATE, q),
                           (F32_SUBLANES, LANES))],
        compiler_params=_cparams(2),
        name="ssd_scan",
    )(xbc, xbc, xbc, dt, p["a_log"], p["d_skip"], h0)


def _ssd_out_kernel(y_ref, z_ref, x_ref, mod_ref, nw_ref, sn_ref, wo_ref, o_ref):
    v = y_ref[0].astype(F32) * _silu(z_ref[0].astype(F32))
    inv = lax.rsqrt(jnp.mean(v * v, axis=-1, keepdims=True) + EPS)
    m = _dot((v * sn_ref[...]).astype(BF16), wo_ref[...]) * inv
    o_ref[0] = x_ref[0] + mod_ref[0, 2:3, :] * _rms(m, nw_ref[1:2, :])


def _ssd_out(y, z, x, mods, nw, p, *, tr):
    nb, seq, d = x.shape
    consts = [nw, p["norm"], p["w_out"]]
    return pl.pallas_call(
        _ssd_out_kernel,
        out_shape=jax.ShapeDtypeStruct((nb, seq, d), F32),
        grid=(nb, seq // tr),
        in_specs=[_row_spec(tr, y.shape[2]), _row_spec(tr, z.shape[2]), _row_spec(tr, d),
                  pl.BlockSpec((1, 6, d), lambda b, i: (b, 0, 0))] + [_const_spec(a.shape) for a in consts],
        out_specs=_row_spec(tr, d),
        compiler_params=_cparams(2),
        name="ssd_out",
    )(y, z, x, mods, *consts)


def _prep_hyb(w_in, q_norm, kv_norm, w_uq, w_ukv, sconv_w, w_out):
    d = w_in.shape[0]
    o_kr = MLA_Q_RANK + MLA_KV_RANK
    kr_block = jnp.pad(w_in[:, o_kr:o_kr + MLA_ROPE], ((0, 0), (MLA_NOPE, HEAD_PAD - MLA_NOPE - MLA_ROPE)))
    w_in_p = jnp.concatenate([w_in[:, :o_kr], kr_block, w_in[:, o_kr + MLA_ROPE:]], axis=1)
    pad_heads = lambda w, width: jnp.pad(w, ((0, 0), (0, 0), (0, HEAD_PAD - width))).reshape(w.shape[0], -1)
    qk = MLA_NOPE + MLA_ROPE
    w_uq_p = pad_heads(w_uq.reshape(MLA_Q_RANK, MLA_HEADS, qk), qk)
    w_ukv_h = w_ukv.reshape(MLA_KV_RANK, MLA_HEADS, MLA_NOPE + MLA_V)
    w_k = pad_heads(w_ukv_h[..., :MLA_NOPE], MLA_NOPE)
    w_v = pad_heads(w_ukv_h[..., MLA_NOPE:], MLA_V)
    n_attn = MLA_HEADS * MLA_V
    w_oa = jnp.pad(w_out[:n_attn].reshape(MLA_HEADS, MLA_V, d), ((0, 0), (0, HEAD_PAD - MLA_V), (0, 0)))
    return dict(w_in=w_in_p.astype(BF16), q_norm=q_norm[None, :], kv_norm=kv_norm[None, :],
                w_uq=w_uq_p.astype(BF16), w_k=w_k.astype(BF16), w_vt=w_v.T.astype(BF16),
                sconv_w=sconv_w, w_oa=w_oa.reshape(MLA_HEADS * HEAD_PAD, d).astype(BF16),
                w_os=w_out[n_attn:].astype(BF16))


def _prep_ssd(w_in, conv_w, conv_b, a_log, dt_bias, d_skip, norm_w, w_out):
    d = w_in.shape[0]
    inner = norm_w.shape[0]
    heads = a_log.shape[1]
    hpg = heads // SSD_GROUPS
    n_xbc = conv_w.shape[1]
    per_group = lambda t: jnp.pad(t.reshape(-1, 2, SSD_GROUPS, hpg).transpose(0, 2, 1, 3).reshape(-1, SSD_GROUPS, 2 * hpg),
                                  ((0, 0), (0, 0), (0, LANES - 2 * hpg)))
    w_dt = per_group(w_in[:, inner + n_xbc:]).reshape(d, SSD_GROUPS * LANES)
    return dict(w_z=w_in[:, :inner].astype(BF16), w_xbc=w_in[:, inner:inner + n_xbc].astype(BF16),
                w_dt=w_dt.astype(BF16), conv_w=conv_w, conv_b=conv_b[None, :],
                dt_bias=per_group(dt_bias.reshape(1, -1)).reshape(1, SSD_GROUPS * LANES),
                a_log=per_group(a_log.reshape(1, -1))[0][:, None, :],
                d_skip=jnp.repeat(d_skip.reshape(SSD_GROUPS, 1, hpg), SSD_HEADDIM, axis=2),
                norm=norm_w[None, :], w_out=w_out.astype(BF16))


def _rope_tables(rows):
    row = jnp.repeat(jnp.arange(rows, dtype=F32), GRID_W)
    col = jnp.tile(jnp.arange(GRID_W, dtype=F32), rows)
    nf = MLA_ROPE // 4
    inv = ROPE_THETA ** (-jnp.arange(nf, dtype=F32) / nf)
    ang = jnp.concatenate([row[:, None] * inv, col[:, None] * inv], axis=-1)
    cos, sin = jnp.cos(ang), jnp.sin(ang)
    n = ang.shape[0]
    half = MLA_ROPE // 2
    ones, zeros = jnp.ones((n, MLA_NOPE), F32), jnp.zeros((n, MLA_NOPE), F32)
    tail1, tail0 = jnp.ones((n, HEAD_PAD - MLA_NOPE - MLA_ROPE), F32), jnp.zeros((n, HEAD_PAD - MLA_NOPE - MLA_ROPE), F32)
    zh = jnp.zeros((n, half), F32)
    return (jnp.concatenate([ones, cos, cos, tail1], axis=1),
            jnp.concatenate([zeros, -sin, zh, tail0], axis=1),
            jnp.concatenate([zeros, zh, sin, tail0], axis=1))


def _identity_tables(n):
    return jnp.ones((n, HEAD_PAD), F32), jnp.zeros((n, HEAD_PAD), F32), jnp.zeros((n, HEAD_PAD), F32)


def _row_tile(seq):
    return min(seq, 512)


def kernel(x, c, ctx, c_ctx, mod_w, mod_b, norm_w, ffn_w_up, ffn_conv_w, ffn_w_down, hyb_w_in, mla_q_norm, mla_kv_norm, mla_w_uq, mla_w_ukv, sconv_w, hyb_w_out, ssd_w_in, ssd_conv_w, ssd_conv_b, ssd_a_log, ssd_dt_bias, ssd_d, ssd_norm, ssd_w_out):
    nb, seq, d = x.shape
    n_ctx = ctx.shape[1]
    depth = mod_w.shape[0]
    tr_l, tr_c = _row_tile(seq), _row_tile(n_ctx)
    tq_l, tq_c = min(seq, 2048), min(n_ctx, 256)

    n_rows = -(-(nb + 1) // F32_SUBLANES) * F32_SUBLANES
    cond = jnp.concatenate([c, c_ctx[None, :], jnp.zeros((n_rows - nb - 1, d), F32)], axis=0)
    mods = _modulation(cond, mod_w, mod_b).reshape(depth, n_rows, 6, d)

    tabs_l = _rope_tables(seq // GRID_W)
    tabs_c = _identity_tables(n_ctx)
    heads = ssd_a_log.shape[2]
    h0 = jnp.zeros((nb, SSD_GROUPS, 2, SSD_STATE, heads // SSD_GROUPS * SSD_HEADDIM), F32)

    ffn_up, ffn_down = ffn_w_up.astype(BF16), ffn_w_down.astype(BF16)
    for l in range(depth):
        last = l == depth - 1
        i = l // 2
        mod_l = mods[l, :nb]
        mod_c = jnp.broadcast_to(mods[l, nb:nb + 1], (nb, 6, d))
        nw = norm_w[l]
        if l % 2 == 0:
            p = _prep_hyb(hyb_w_in[i], mla_q_norm[i], mla_kv_norm[i], mla_w_uq[i], mla_w_ukv[i], sconv_w[i],
                          hyb_w_out[i])
            q_l, k_l, v_l, gb_l, u_l = _hyb_in(x, mod_l, nw, p, tabs_l, tr=tr_l)
            q_c, k_c, v_c, gb_c, u_c = _hyb_in(ctx, mod_c, nw, p, tabs_c, tr=tr_c)
            attn_l = _attention(q_l, [(k_c, v_c), (k_l, v_l)], tq=tq_l)
            x = _hyb_out(attn_l, gb_l, u_l, x, mod_l, nw, p, tr=tr_l)
            if not last:
                attn_c = _attention(q_c, [(k_c, v_c)], tq=tq_c)
                ctx = _hyb_out(attn_c, gb_c, u_c, ctx, mod_c, nw, p, tr=tr_c)
        else:
            p = _prep_ssd(ssd_w_in[i], ssd_conv_w[i], ssd_conv_b[i], ssd_a_log[i], ssd_dt_bias[i], ssd_d[i],
                          ssd_norm[i], ssd_w_out[i])
            z_c, xbc_c, dt_c = _ssd_in(ctx, mod_c, nw, p, tr=tr_c)
            y_c, h_fin = _ssd_scan(xbc_c, dt_c, h0, p)
            z_l, xbc_l, dt_l = _ssd_in(x, mod_l, nw, p, tr=tr_l)
            y_l, _ = _ssd_scan(xbc_l, dt_l, h_fin, p)
            x = _ssd_out(y_l, z_l, x, mod_l, nw, p, tr=tr_l)
            if not last:
                ctx = _ssd_out(y_c, z_c, ctx, mod_c, nw, p, tr=tr_c)
        x = _ffn(x, mod_l, nw, ffn_up, ffn_conv_w, ffn_down, layer=l, tr=tr_l, ck=FFN_CHUNK)
        if not last:
            ctx = _ffn(ctx, mod_c, nw, ffn_up, ffn_conv_w, ffn_down, layer=l, tr=tr_c, ck=FFN_CHUNK)
    return x
```
